```python
import math
import jax, jax.numpy as jnp
from jax import lax
import numpy as np

D_MODEL = 1024
BATCH = 4
SEQ = 4096
DEPTH = 4

D_MIX = D_MODEL
W_GROUP = D_MIX // 4
EPS = 1e-6
CONV_A_WIDTH = 31
A_GROUPS = 4
B_HEADS = 4
B_KV_HEADS = 2
B_HEAD_DIM = W_GROUP // B_HEADS
WINDOW = 128
BLOCK = 128
N_BUCKETS = 32
MAX_DISTANCE = 128
CONV_C_WIDTH = 4
C_BLOCKS = 4
C_BLOCK_DIM = W_GROUP // C_BLOCKS
LRU_C = 8.0
D_HEADS = 4
D_KEY_DIM = (W_GROUP // 2) // D_HEADS
D_VAL_DIM = W_GROUP // D_HEADS
GATE_RANK = 16
GATE_TAU = 16.0
CHUNK = 64

SPLIT_SIZES = [
    W_GROUP, W_GROUP, W_GROUP,
    B_HEADS * B_HEAD_DIM, B_KV_HEADS * B_HEAD_DIM, B_KV_HEADS * B_HEAD_DIM, W_GROUP,
    W_GROUP, W_GROUP,
    D_HEADS * D_KEY_DIM, D_HEADS * D_KEY_DIM, D_HEADS * D_VAL_DIM, GATE_RANK, W_GROUP,
]
IN_COLS = int(sum(SPLIT_SIZES))
SPLIT_POINTS = [int(v) for v in np.cumsum(SPLIT_SIZES)[:-1]]

kernel_name = "hymba_style_four_mixer_hybrid"


def rmsnorm(x, g):
    xf = x.astype(jnp.float32)
    y = xf * lax.rsqrt(jnp.mean(xf * xf, axis=-1, keepdims=True) + EPS)
    return (y * g.astype(jnp.float32)).astype(x.dtype)


def causal_dwconv(x, w, b):
    k = w.shape[0]
    y = lax.conv_general_dilated(x, w[:, None, :].astype(x.dtype), window_strides=(1,),
                                 padding=[(k - 1, 0)], dimension_numbers=("NWC", "WIO", "NWC"),
                                 feature_group_count=x.shape[-1])
    return y + b.astype(x.dtype)


def conformer_conv(val, glu, w_dw, b_dw, ln_g, ln_b, w_pw):
    u = val * jax.nn.sigmoid(glu)
    u = causal_dwconv(u, w_dw, b_dw)
    bsz, t, c = u.shape
    ug = u.reshape(bsz, t, A_GROUPS, c // A_GROUPS).astype(jnp.float32)
    mu = jnp.mean(ug, axis=-1, keepdims=True)
    var = jnp.mean(jnp.square(ug - mu), axis=-1, keepdims=True)
    ug = ((ug - mu) * lax.rsqrt(var + EPS)).reshape(bsz, t, c)
    u = (ug * ln_g.astype(jnp.float32) + ln_b.astype(jnp.float32)).astype(val.dtype)
    return jax.nn.silu(u) @ w_pw


def t5_bucket(dist):
    max_exact = N_BUCKETS // 2
    d = jnp.maximum(dist, 1).astype(jnp.float32)
    large = max_exact + (jnp.log(d / max_exact) / math.log(MAX_DISTANCE / max_exact)
                         * (N_BUCKETS - max_exact)).astype(jnp.int32)
    large = jnp.minimum(large, N_BUCKETS - 1)
    return jnp.where(dist < max_exact, dist, large)


def swa_sink_attention(q, k, v, q_g, k_g, sinks, bias):
    bsz, t, _ = q.shape
    nb = t // BLOCK
    grp = B_HEADS // B_KV_HEADS
    q = rmsnorm(q.reshape(bsz, t, B_KV_HEADS, grp, B_HEAD_DIM), q_g)
    k = rmsnorm(k.reshape(bsz, t, B_KV_HEADS, B_HEAD_DIM), k_g)
    v = v.reshape(bsz, t, B_KV_HEADS, B_HEAD_DIM)
    qb = q.reshape(bsz, nb, BLOCK, B_KV_HEADS, grp, B_HEAD_DIM)

    def band(z):
        prev = jnp.pad(z, ((0, 0), (BLOCK, 0), (0, 0), (0, 0)))[:, :t]
        shp = (bsz, nb, BLOCK, B_KV_HEADS, B_HEAD_DIM)
        return jnp.concatenate([prev.reshape(shp), z.reshape(shp)], axis=2)

    kb, vb = band(k), band(v)
    s = jnp.einsum('bnqhgd,bnkhd->bnhgqk', qb, kb,
                   preferred_element_type=jnp.float32) * (B_HEAD_DIM ** -0.5)
    s = s + bias.reshape(B_KV_HEADS, grp, BLOCK, 2 * BLOCK).astype(jnp.float32)[None, None]
    qi = jnp.arange(BLOCK)[:, None]
    kj = jnp.arange(2 * BLOCK)[None, :]
    dist = qi + BLOCK - kj
    keypos = jnp.arange(nb)[:, None, None] * BLOCK - BLOCK + kj[None]
    valid = (dist >= 0)[None] & (dist < WINDOW)[None] & (keypos >= 0)
    s = jnp.where(valid[None, :, None, None], s, -jnp.inf)
    sink = sinks.reshape(B_KV_HEADS, grp).astype(jnp.float32)[None, None, :, :, None, None]
    m = jnp.maximum(jnp.max(s, axis=-1, keepdims=True), sink)
    p = jnp.exp(s - m)
    p = p / (jnp.sum(p, axis=-1, keepdims=True) + jnp.exp(sink - m))
    o = jnp.einsum('bnhgqk,bnkhd->bnqhgd', p.astype(v.dtype), vb)
    return o.reshape(bsz, t, B_HEADS * B_HEAD_DIM)


def rg_lru(xc, conv_w, conv_b, w_r, b_r, w_i, b_i, lam):
    x = causal_dwconv(xc, conv_w, conv_b)
    bsz, t, c = x.shape
    xb = x.reshape(bsz, t, C_BLOCKS, C_BLOCK_DIM)
    r = jax.nn.sigmoid(jnp.einsum('btnc,ncd->btnd', xb, w_r).reshape(bsz, t, c) + b_r)
    i = jax.nn.sigmoid(jnp.einsum('btnc,ncd->btnd', xb, w_i).reshape(bsz, t, c) + b_i)
    log_a = (-LRU_C * r.astype(jnp.float32)) * jax.nn.softplus(-lam.astype(jnp.float32))
    a = jnp.exp(log_a)
    u = jnp.sqrt(-jnp.expm1(2.0 * log_a)) * (i * x).astype(jnp.float32)

    def combine(left, right):
        a1, b1 = left
        a2, b2 = right
        return a1 * a2, a2 * b1 + b2

    _, h = lax.associative_scan(combine, (a, u), axis=1)
    return h.astype(x.dtype)


def gla(q, k, v, lr, w_up, b_up, norm_g):
    bsz, t, _ = q.shape
    nc = t // CHUNK
    f32 = jnp.float32
    g = jax.nn.log_sigmoid((lr @ w_up + b_up).astype(f32)) / GATE_TAU
    q = q.reshape(bsz, nc, CHUNK, D_HEADS, D_KEY_DIM).astype(f32) * (D_KEY_DIM ** -0.5)
    k = k.reshape(bsz, nc, CHUNK, D_HEADS, D_KEY_DIM).astype(f32)
    vv = v.reshape(bsz, nc, CHUNK, D_HEADS, D_VAL_DIM).astype(f32)
    g = g.reshape(bsz, nc, CHUNK, D_HEADS, D_KEY_DIM)
    b = jnp.cumsum(g, axis=2)
    b_last = b[:, :, -1:]
    q_t = q * jnp.exp(b)
    k_t = k * jnp.exp(-b)
    k_end = k * jnp.exp(b_last - b)
    causal = jnp.tril(jnp.ones((CHUNK, CHUNK), f32))
    att = jnp.einsum('bnihd,bnjhd->bnhij', q_t, k_t) * causal
    o_intra = jnp.einsum('bnhij,bnjhe->bnihe', att, vv)
    d_state = jnp.einsum('bnjhd,bnjhe->bnhde', k_end, vv)
    decay = jnp.exp(b_last[:, :, 0])

    def step(state, inp):
        ds, dec = inp
        return state * dec[..., None] + ds, state

    s0 = jnp.zeros((bsz, D_HEADS, D_KEY_DIM, D_VAL_DIM), f32)
    _, s_prev = lax.scan(step, s0, (jnp.moveaxis(d_state, 1, 0), jnp.moveaxis(decay, 1, 0)))
    s_prev = jnp.moveaxis(s_prev, 0, 1)
    o_inter = jnp.einsum('bnihd,bnhde->bnihe', q_t, s_prev)
    o = (o_intra + o_inter).reshape(bsz, t, D_HEADS, D_VAL_DIM)
    o = rmsnorm(o, norm_g)
    return o.reshape(bsz, t, D_HEADS * D_VAL_DIM).astype(v.dtype)


def setup_inputs(seed: int = 0) -> dict:
    key = jax.random.key(seed)
    ks = jax.random.split(key, 24)
    f32 = jnp.float32
    nrm = lambda k, shp, s: jax.random.normal(k, shp, f32) * s
    a0 = jax.random.uniform(ks[20], (DEPTH, W_GROUP), f32, 0.9, 0.999)
    base = a0 ** (1.0 / LRU_C)
    return {
        "x": jax.random.normal(ks[0], (BATCH, SEQ, D_MODEL), f32),
        "norm_g": 1.0 + nrm(ks[1], (DEPTH, D_MODEL), 0.02),
        "w_in": nrm(ks[2], (DEPTH, D_MODEL, IN_COLS), D_MODEL ** -0.5),
        "a_conv_w": nrm(ks[3], (DEPTH, CONV_A_WIDTH, W_GROUP), CONV_A_WIDTH ** -0.5),
        "a_conv_b": nrm(ks[4], (DEPTH, W_GROUP), 0.01),
        "a_ln_g": 1.0 + nrm(ks[5], (DEPTH, W_GROUP), 0.02),
        "a_ln_b": nrm(ks[6], (DEPTH, W_GROUP), 0.01),
        "a_pw": nrm(ks[7], (DEPTH, W_GROUP, W_GROUP), W_GROUP ** -0.5),
        "b_q_g": 1.0 + nrm(ks[8], (DEPTH, B_HEAD_DIM), 0.02),
        "b_k_g": 1.0 + nrm(ks[9], (DEPTH, B_HEAD_DIM), 0.02),
        "b_sinks": nrm(ks[10], (DEPTH, B_HEADS), 0.5),
        "rel_bias": nrm(ks[11], (N_BUCKETS, B_HEADS), 0.5),
        "c_conv_w": nrm(ks[12], (DEPTH, CONV_C_WIDTH, W_GROUP), CONV_C_WIDTH ** -0.5),
        "c_conv_b": nrm(ks[13], (DEPTH, W_GROUP), 0.01),
        "c_w_r": nrm(ks[14], (DEPTH, C_BLOCKS, C_BLOCK_DIM, C_BLOCK_DIM), C_BLOCK_DIM ** -0.5),
        "c_b_r": nrm(ks[15], (DEPTH, W_GROUP), 0.01),
        "c_w_i": nrm(ks[16], (DEPTH, C_BLOCKS, C_BLOCK_DIM, C_BLOCK_DIM), C_BLOCK_DIM ** -0.5),
        "c_b_i": nrm(ks[17], (DEPTH, W_GROUP), 0.01),
        "c_lambda": jnp.log(base) - jnp.log1p(-base),
        "d_w_up": nrm(ks[18], (DEPTH, GATE_RANK, D_HEADS * D_KEY_DIM), GATE_RANK ** -0.5),
        "d_b_up": nrm(ks[19], (DEPTH, D_HEADS * D_KEY_DIM), 0.01),
        "d_norm_g": 1.0 + nrm(ks[21], (DEPTH, D_VAL_DIM), 0.02),
        "w_out": nrm(ks[22], (DEPTH, D_MIX, D_MODEL), 0.5 * D_MIX ** -0.5),
    }


def reference(x, norm_g, w_in, a_conv_w, a_conv_b, a_ln_g, a_ln_b, a_pw, b_q_g, b_k_g, b_sinks,
              rel_bias, c_conv_w, c_conv_b, c_w_r, c_b_r, c_w_i, c_b_i, c_lambda, d_w_up, d_b_up,
              d_norm_g, w_out):
    dist = jnp.arange(BLOCK)[:, None] + BLOCK - jnp.arange(2 * BLOCK)[None, :]
    bucket = t5_bucket(jnp.clip(dist, 0, None))
    bias = jnp.transpose(rel_bias[bucket], (2, 0, 1))
    for l in range(DEPTH):
        h = rmsnorm(x, norm_g[l])
        proj = h @ w_in[l]
        (a_val, a_glu, a_gate, bq, bk, bv, b_gate, c_x, c_gate,
         dq, dk, dv, d_lr, d_gate) = jnp.split(proj, SPLIT_POINTS, axis=-1)
        ya = conformer_conv(a_val, a_glu, a_conv_w[l], a_conv_b[l], a_ln_g[l], a_ln_b[l], a_pw[l]) * jax.nn.silu(a_gate)
        yb = swa_sink_attention(bq, bk, bv, b_q_g[l], b_k_g[l], b_sinks[l], bias) * jax.nn.silu(b_gate)
        yc = rg_lru(c_x, c_conv_w[l], c_conv_b[l], c_w_r[l], c_b_r[l], c_w_i[l], c_b_i[l], c_lambda[l]) * jax.nn.silu(c_gate)
        yd = gla(dq, dk, dv, d_lr, d_w_up[l], d_b_up[l], d_norm_g[l]) * jax.nn.silu(d_gate)
        y = jnp.concatenate([ya, yb, yc, yd], axis=-1)
        x = x + y @ w_out[l]
    return x
```

```python
import functools
import math

import jax
import jax.numpy as jnp
from jax import lax
from jax.experimental import pallas as pl
from jax.experimental.pallas import tpu as pltpu

F32 = jnp.float32
BF16 = jnp.bfloat16

D_MODEL = 1024
DEPTH = 4
W = 256
EPS = 1e-6
CONV_A = 31
A_PAD = 32
HEAD = 64
WINDOW = 128
QBLK = 128
N_BUCKETS = 32
MAX_DISTANCE = 128
CONV_C = 4
C_PAD = 8
LRU_C = 8.0
D_HEADS = 4
D_KEY = 32
GATE_RANK = 16
GATE_TAU = 16.0
CHUNK = 64
SUBLANES = 8
LANES = 128

TB = 256

A_OFF, A_WID = 0, 768
B_OFF, B_WID = 768, 768
C_OFF, C_WID = 1536, 512
D_OFF, D_WID = 2048, 896
IN_PAD = D_OFF + D_WID

R_ACB, R_ALG, R_ALB, R_CCB, R_CBR, R_CBI, R_LAM, R_DNG, R_CCW, R_BQG = 0, 1, 2, 3, 4, 5, 6, 7, 8, 12


def _dot(a, b):
    return jnp.dot(a, b, preferred_element_type=F32)


def _dot_nt(a, b):
    return lax.dot_general(a, b, (((1,), (1,)), ((), ())), preferred_element_type=F32)


def _dot_tn(a, b):
    return lax.dot_general(a, b, (((0,), (0,)), ((), ())), preferred_element_type=F32)


def _dot_split(x, w_bf, passes):
    acc = None
    r = x
    for p in range(passes):
        part = r.astype(BF16)
        d = _dot(part, w_bf)
        acc = d if acc is None else acc + d
        if p + 1 < passes:
            r = r - part.astype(F32)
    return acc


def _sigmoid(x):
    return 1.0 / (1.0 + jnp.exp(-x))


def _silu(x):
    return x * _sigmoid(x)


def _layer_kernel(x_ref, ng_ref, win_ref, wout_ref, apw_ref, cwr_ref, cwi_ref, dwup_ref, aconv_ref,
                  p256_ref, p128_ref, bucket_ref, sinks_ref, relb_ref,
                  o_ref,
                  ubuf, cbuf, hcar, kbuf, vbuf, st, ycat, biasm, g256, tri, sa, su):
    b_idx = pl.program_id(0)
    t_idx = pl.program_id(1)

    @pl.when((b_idx == 0) & (t_idx == 0))
    def _build_tables():
        ri = lax.broadcasted_iota(jnp.int32, (W, W), 0) // HEAD
        ci = lax.broadcasted_iota(jnp.int32, (W, W), 1) // HEAD
        g256[...] = jnp.where(ri == ci, 1.0 / HEAD, 0.0).astype(BF16)
        rt = lax.broadcasted_iota(jnp.int32, (TB, TB), 0)
        ct = lax.broadcasted_iota(jnp.int32, (TB, TB), 1)
        tri[...] = jnp.where((rt // CHUNK == ct // CHUNK) & (rt >= ct), 1.0, 0.0).astype(BF16)
        bucket = bucket_ref[...]
        qi = lax.broadcasted_iota(jnp.int32, (QBLK, 2 * QBLK), 0)
        kj = lax.broadcasted_iota(jnp.int32, (QBLK, 2 * QBLK), 1)
        dist = qi + QBLK - kj
        valid = (dist >= 0) & (dist < WINDOW)
        for h in range(4):
            acc = jnp.zeros((QBLK, 2 * QBLK), F32)
            for bk in range(N_BUCKETS):
                acc = jnp.where(bucket == bk, relb_ref[bk * 4 + h], acc)
            biasm[h * QBLK:(h + 1) * QBLK, :] = jnp.where(valid, acc, -jnp.inf)

    @pl.when(t_idx == 0)
    def _reset_state():
        ubuf[0:A_PAD, :] = jnp.zeros((A_PAD, W), F32)
        cbuf[0:C_PAD, :] = jnp.zeros((C_PAD, W), F32)
        hcar[...] = jnp.zeros((SUBLANES, W), F32)
        kbuf[0:QBLK, :] = jnp.zeros((QBLK, LANES), BF16)
        vbuf[0:QBLK, :] = jnp.zeros((QBLK, LANES), BF16)
        st[...] = jnp.zeros((W, LANES), F32)

    x = x_ref[0]
    ms = jnp.mean(x * x, axis=-1, keepdims=True)
    hb = (x * lax.rsqrt(ms + EPS) * ng_ref[...]).astype(BF16)

    gmat = g256[...]

    pa = _dot(hb, win_ref[:, A_OFF:A_OFF + A_WID])
    u = pa[:, 0:W] * _sigmoid(pa[:, W:2 * W])
    ubuf[A_PAD:A_PAD + TB, :] = u
    conv = jnp.broadcast_to(p256_ref[R_ACB:R_ACB + 1, :], (TB, W))
    for j in range(CONV_A):
        off = A_PAD - (CONV_A - 1) + j
        conv = conv + aconv_ref[j:j + 1, :] * ubuf[pl.ds(off, TB), :]
    ubuf[0:A_PAD, :] = ubuf[TB:TB + A_PAD, :]
    mu = _dot_split(conv, gmat, 2)
    dc = conv - mu
    var = _dot_split(dc * dc, gmat, 2)
    un = dc * lax.rsqrt(var + EPS) * p256_ref[R_ALG:R_ALG + 1, :] + p256_ref[R_ALB:R_ALB + 1, :]
    ya = _dot(_silu(un).astype(BF16), apw_ref[...]) * _silu(pa[:, 2 * W:3 * W])
    ycat[:, 0:W] = ya.astype(BF16)

    pb = _dot(hb, win_ref[:, B_OFF:B_OFF + B_WID])
    q = pb[:, 0:W]
    k = pb[:, W:W + LANES]
    v = pb[:, W + LANES:W + 2 * LANES]
    qn = q * lax.rsqrt(_dot_split(q * q, gmat, 2) + EPS) * p256_ref[R_BQG:R_BQG + 1, :]
    kn = k * lax.rsqrt(_dot_split(k * k, gmat[0:LANES, 0:LANES], 2) + EPS) * p128_ref[0:1, :]
    kbuf[QBLK:QBLK + TB, :] = kn.astype(BF16)
    vbuf[QBLK:QBLK + TB, :] = v.astype(BF16)
    lane = lax.broadcasted_iota(jnp.int32, (1, LANES), 1)
    lo = (lane < HEAD).astype(F32)
    hi = 1.0 - lo
    first_tile_mask = jnp.where(t_idx == 0, -jnp.inf, 0.0)
    scale = HEAD ** -0.5
    for bi in range(TB // QBLK):
        r0 = bi * QBLK
        qa = qn[r0:r0 + QBLK, 0:LANES]
        qb = qn[r0:r0 + QBLK, LANES:2 * LANES]
        qst = jnp.concatenate([qa * lo, qb * lo, qa * hi, qb * hi], axis=0).astype(BF16)
        kb = kbuf[r0:r0 + 2 * QBLK, :]
        vb = vbuf[r0:r0 + 2 * QBLK, :]
        s = _dot_nt(qst, kb) * scale + biasm[...]
        outs = []
        for h in range(4):
            sh = s[h * QBLK:(h + 1) * QBLK, :]
            if bi == 0:
                sh = jnp.concatenate([sh[:, 0:QBLK] + first_tile_mask, sh[:, QBLK:]], axis=1)
            sink = sinks_ref[h]
            m = jnp.maximum(jnp.max(sh, axis=-1, keepdims=True), sink)
            p = jnp.exp(sh - m)
            den = jnp.sum(p, axis=-1, keepdims=True) + jnp.exp(sink - m)
            outs.append((p.astype(BF16), den))
        pst = jnp.concatenate([o[0] for o in outs], axis=0)
        ov = _dot(pst, vb)
        oh = [ov[h * QBLK:(h + 1) * QBLK, :] / outs[h][1] for h in range(4)]
        lanem = lane < HEAD
        yb_blk = jnp.concatenate([jnp.where(lanem, oh[0], oh[2]), jnp.where(lanem, oh[1], oh[3])], axis=1)
        gate = pb[r0:r0 + QBLK, W + 2 * LANES:2 * W + 2 * LANES]
        ycat[r0:r0 + QBLK, W:2 * W] = (yb_blk * _silu(gate)).astype(BF16)
    kbuf[0:QBLK, :] = kbuf[TB:TB + QBLK, :]
    vbuf[0:QBLK, :] = vbuf[TB:TB + QBLK, :]

    pc = _dot(hb, win_ref[:, C_OFF:C_OFF + C_WID])
    cbuf[C_PAD:C_PAD + TB, :] = pc[:, 0:W]
    xc = jnp.broadcast_to(p256_ref[R_CCB:R_CCB + 1, :], (TB, W))
    for j in range(CONV_C):
        off = C_PAD - (CONV_C - 1) + j
        xc = xc + p256_ref[R_CCW + j:R_CCW + j + 1, :] * cbuf[pl.ds(off, TB), :]
    cbuf[0:C_PAD, :] = cbuf[TB:TB + C_PAD, :]
    xcb = xc.astype(BF16)
    rg = _sigmoid(_dot(xcb, cwr_ref[...]) + p256_ref[R_CBR:R_CBR + 1, :])
    ig = _sigmoid(_dot(xcb, cwi_ref[...]) + p256_ref[R_CBI:R_CBI + 1, :])
    nlam = -p256_ref[R_LAM:R_LAM + 1, :]
    softplus = jnp.maximum(nlam, 0.0) + jnp.log1p(jnp.exp(-jnp.abs(nlam)))
    log_a = (-LRU_C * rg) * softplus
    a = jnp.exp(log_a)
    uu = jnp.sqrt(1.0 - jnp.exp(2.0 * log_a)) * (ig * xc)
    sa[0:SUBLANES, :] = jnp.ones((SUBLANES, W), F32)
    su[0:SUBLANES, :] = jnp.zeros((SUBLANES, W), F32)
    step = 1
    while step < TB:
        if step < SUBLANES:
            sa[SUBLANES:SUBLANES + TB, :] = a
            su[SUBLANES:SUBLANES + TB, :] = uu
            a_sh = sa[pl.ds(SUBLANES - step, TB), :]
            u_sh = su[pl.ds(SUBLANES - step, TB), :]
        else:
            a_sh = jnp.concatenate([jnp.ones((step, W), F32), a[0:TB - step]], axis=0)
            u_sh = jnp.concatenate([jnp.zeros((step, W), F32), uu[0:TB - step]], axis=0)
        uu = a * u_sh + uu
        a = a * a_sh
        step *= 2
    hs = uu + a * hcar[0:1, :]
    hcar[0:1, :] = hs[TB - 1:TB, :]
    ycat[:, 2 * W:3 * W] = (hs * _silu(pc[:, W:2 * W])).astype(BF16)

    pd = _dot(hb, win_ref[:, D_OFF:D_OFF + D_WID])
    dq = pd[:, 0:LANES] * (D_KEY ** -0.5)
    dk = pd[:, LANES:2 * LANES]
    dv = pd[:, 2 * LANES:2 * LANES + W]
    dgate = pd[:, 2 * LANES + W:2 * LANES + 2 * W]
    dlr = pd[:, 2 * LANES + 2 * W:3 * LANES + 2 * W]
    z = _dot(dlr.astype(BF16), dwup_ref[...]) + p128_ref[1:2, :]
    lg = (jnp.minimum(z, 0.0) - jnp.log1p(jnp.exp(-jnp.abs(z)))) * (1.0 / GATE_TAU)
    bcum = None
    r = lg
    for p_ in range(3):
        part = r.astype(BF16)
        d_ = _dot(tri[...], part)
        bcum = d_ if bcum is None else bcum + d_
        r = r - part.astype(F32)
    lane256 = lax.broadcasted_iota(jnp.int32, (1, W), 1)
    ci = lax.broadcasted_iota(jnp.int32, (CHUNK, W), 0)
    cj = lax.broadcasted_iota(jnp.int32, (CHUNK, W), 1) % CHUNK
    causal = (ci >= cj).astype(F32)
    bdm = (lax.broadcasted_iota(jnp.int32, (W, LANES), 0) // HEAD
           == lax.broadcasted_iota(jnp.int32, (W, LANES), 1) // D_KEY).astype(F32)
    state = st[...]
    od_parts = []
    for c in range(TB // CHUNK):
        r0 = c * CHUNK
        bc = bcum[r0:r0 + CHUNK, :]
        bl = bc[CHUNK - 1:CHUNK, :]
        qt = (dq[r0:r0 + CHUNK, :] * jnp.exp(bc)).astype(BF16)
        kc = dk[r0:r0 + CHUNK, :]
        kt = kc * jnp.exp(-bc)
        ke = (kc * jnp.exp(bl - bc)).astype(BF16)
        vc = dv[r0:r0 + CHUNK, :]
        kst = jnp.concatenate([kt * (lane // D_KEY == h).astype(F32) for h in range(D_HEADS)],
                              axis=0).astype(BF16)
        att = (_dot_nt(qt, kst) * causal).astype(BF16)
        vbd = jnp.concatenate([vc * (lane256 // HEAD == h).astype(F32) for h in range(D_HEADS)],
                              axis=0).astype(BF16)
        o_c = _dot(att, vbd) + _dot_nt(qt, state.astype(BF16))
        state = state * jnp.exp(bl) + _dot_tn(vc.astype(BF16), ke) * bdm
        od_parts.append(o_c)
    st[...] = state
    od = jnp.concatenate(od_parts, axis=0)
    odn = od * lax.rsqrt(_dot_split(od * od, gmat, 2) + EPS) * p256_ref[R_DNG:R_DNG + 1, :]
    ycat[:, 3 * W:4 * W] = (odn * _silu(dgate)).astype(BF16)

    o_ref[0] = x + _dot(ycat[...], wout_ref[...])


def _t5_bucket(dist):
    max_exact = N_BUCKETS // 2
    d = jnp.maximum(dist, 1).astype(F32)
    large = max_exact + (jnp.log(d / max_exact) / math.log(MAX_DISTANCE / max_exact)
                         * (N_BUCKETS - max_exact)).astype(jnp.int32)
    large = jnp.minimum(large, N_BUCKETS - 1)
    return jnp.where(dist < max_exact, dist, large)


def _full(shape):
    return pl.BlockSpec(shape, lambda b, t: (0,) * len(shape))


def _layer_call(x, ng, win, wout, apw, cwr, cwi, dwup, aconv, p256, p128, bucket, sinks, relb):
    bsz, seq, _ = x.shape
    smem = pl.BlockSpec(memory_space=pltpu.SMEM)
    return pl.pallas_call(
        _layer_kernel,
        grid=(bsz, seq // TB),
        in_specs=[
            pl.BlockSpec((1, TB, D_MODEL), lambda b, t: (b, t, 0)),
            _full((1, D_MODEL)), _full((D_MODEL, IN_PAD)), _full((D_MODEL, D_MODEL)),
            _full((W, W)), _full((W, W)), _full((W, W)), _full((LANES, LANES)),
            _full((A_PAD, W)), _full((16, W)), _full((SUBLANES, LANES)), _full((QBLK, 2 * QBLK)),
            smem, smem,
        ],
        out_specs=pl.BlockSpec((1, TB, D_MODEL), lambda b, t: (b, t, 0)),
        out_shape=jax.ShapeDtypeStruct(x.shape, F32),
        scratch_shapes=[
            pltpu.VMEM((TB + A_PAD, W), F32),
            pltpu.VMEM((TB + C_PAD, W), F32),
            pltpu.VMEM((SUBLANES, W), F32),
            pltpu.VMEM((TB + QBLK, LANES), BF16),
            pltpu.VMEM((TB + QBLK, LANES), BF16),
            pltpu.VMEM((W, LANES), F32),
            pltpu.VMEM((TB, D_MODEL), BF16),
            pltpu.VMEM((4 * QBLK, 2 * QBLK), F32),
            pltpu.VMEM((W, W), BF16),
            pltpu.VMEM((TB, TB), BF16),
            pltpu.VMEM((TB + SUBLANES, W), F32),
            pltpu.VMEM((TB + SUBLANES, W), F32),
        ],
        compiler_params=pltpu.CompilerParams(
            dimension_semantics=("arbitrary", "arbitrary"),
            vmem_limit_bytes=48 * 1024 * 1024,
        ),
        name="hybrid_layer",
    )(x, ng, win, wout, apw, cwr, cwi, dwup, aconv, p256, p128, bucket, sinks, relb)


def _block_diag(blocks):
    n, c, _ = blocks.shape
    eye = jnp.eye(n, dtype=blocks.dtype)
    return (eye[:, None, :, None] * blocks[:, :, None, :]).reshape(n * c, n * c)


def _head_perm(w, axis):
    parts = jnp.split(w, 4, axis=axis)
    return jnp.concatenate([parts[0], parts[2], parts[1], parts[3]], axis=axis)


def kernel(x, norm_g, w_in, a_conv_w, a_conv_b, a_ln_g, a_ln_b, a_pw, b_q_g, b_k_g, b_sinks, rel_bias,
           c_conv_w, c_conv_b, c_w_r, c_b_r, c_w_i, c_b_i, c_lambda, d_w_up, d_b_up, d_norm_g, w_out):
    dist = jnp.arange(QBLK)[:, None] + QBLK - jnp.arange(2 * QBLK)[None, :]
    bucket = _t5_bucket(jnp.clip(dist, 0, None)).astype(jnp.int32)
    relb = rel_bias.reshape(-1)
    for l in range(DEPTH):
        wi = w_in[l]
        win = jnp.concatenate([
            wi[:, 0:768],
            _head_perm(wi[:, 768:1024], 1), wi[:, 1024:1280], _head_perm(wi[:, 1280:1536], 1),
            wi[:, 1536:2048],
            wi[:, 2048:2560], wi[:, 2576:2832], wi[:, 2560:2576],
            jnp.zeros((D_MODEL, IN_PAD - 2832), F32),
        ], axis=1).astype(BF16)
        wo = w_out[l]
        wout = jnp.concatenate([wo[0:256], _head_perm(wo[256:512], 0), wo[512:1024]], axis=0).astype(BF16)
        dwup = jnp.concatenate([d_w_up[l], jnp.zeros((LANES - GATE_RANK, LANES), F32)], axis=0).astype(BF16)
        aconv = jnp.concatenate([a_conv_w[l], jnp.zeros((A_PAD - CONV_A, W), F32)], axis=0)
        p256 = jnp.concatenate([
            a_conv_b[l][None], a_ln_g[l][None], a_ln_b[l][None], c_conv_b[l][None], c_b_r[l][None],
            c_b_i[l][None], c_lambda[l][None], jnp.tile(d_norm_g[l], 4)[None], c_conv_w[l],
            jnp.tile(b_q_g[l], 4)[None], jnp.zeros((3, W), F32)], axis=0)
        p128 = jnp.concatenate([jnp.tile(b_k_g[l], 2)[None], d_b_up[l][None],
                                jnp.zeros((SUBLANES - 2, LANES), F32)], axis=0)
        x = _layer_call(x, norm_g[l][None], win, wout, a_pw[l].astype(BF16),
                        _block_diag(c_w_r[l]).astype(BF16), _block_diag(c_w_i[l]).astype(BF16),
                        dwup, aconv, p256, p128, bucket, b_sinks[l], relb)
    return x
```

```python
import functools
import math

import jax
import jax.numpy as jnp
from jax import lax
from jax.experimental import pallas as pl
from jax.experimental.pallas import tpu as pltpu

F32 = jnp.float32
BF16 = jnp.bfloat16

D_MODEL = 1024
DEPTH = 4
W = 256
EPS = 1e-6
CONV_A = 31
A_PAD = 32
HEAD = 64
WINDOW = 128
QBLK = 128
N_BUCKETS = 32
MAX_DISTANCE = 128
CONV_C = 4
C_PAD = 8
LRU_C = 8.0
D_HEADS = 4
D_KEY = 32
GATE_RANK = 16
GATE_TAU = 16.0
CHUNK = 64
SUBLANES = 8
LANES = 128

TB = 256

A_OFF, A_WID = 0, 768
B_OFF, B_WID = 768, 768
C_OFF, C_WID = 1536, 512
D_OFF, D_WID = 2048, 896
IN_PAD = D_OFF + D_WID

R_ACB, R_ALG, R_ALB, R_CCB, R_CBR, R_CBI, R_LAM, R_DNG, R_CCW, R_BQG = 0, 1, 2, 3, 4, 5, 6, 7, 8, 12


def _dot(a, b):
    return jnp.dot(a, b, preferred_element_type=F32)


def _dot_nt(a, b):
    return lax.dot_general(a, b, (((1,), (1,)), ((), ())), preferred_element_type=F32)


def _dot_tn(a, b):
    return lax.dot_general(a, b, (((0,), (0,)), ((), ())), preferred_element_type=F32)


def _dot_split(x, w_bf, passes):
    acc = None
    r = x
    for p in range(passes):
        part = r.astype(BF16)
        d = _dot(part, w_bf)
        acc = d if acc is None else acc + d
        if p + 1 < passes:
            r = r - part.astype(F32)
    return acc


def _sigmoid(x):
    return 1.0 / (1.0 + jnp.exp(-x))


def _silu(x):
    return x * _sigmoid(x)


def _causal_taps(acc, buf, w_ref, w_row0, n_taps, pad):
    rows = buf.shape[0]
    for res in range(SUBLANES):
        shifted = None
        for j in range(n_taps):
            off = pad - (n_taps - 1) + j
            if off % SUBLANES != res:
                continue
            if shifted is None:
                shifted = buf if res == 0 else pltpu.roll(buf, rows - res, axis=0)
            base = off - res
            acc = acc + w_ref[w_row0 + j:w_row0 + j + 1, :] * shifted[base:base + TB]
    return acc


def _shift_rows(x, step, fill):
    if step % SUBLANES == 0:
        return jnp.concatenate([jnp.full((step, x.shape[1]), fill, F32), x[0:x.shape[0] - step]], axis=0)
    rolled = pltpu.roll(x, step, axis=0)
    row = lax.broadcasted_iota(jnp.int32, (SUBLANES, x.shape[1]), 0)
    head = jnp.where(row < step, fill, rolled[0:SUBLANES])
    return jnp.concatenate([head, rolled[SUBLANES:]], axis=0)


def _layer_kernel(x_ref, ng_ref, win_ref, wout_ref, apw_ref, cwr_ref, cwi_ref, dwup_ref, aconv_ref,
                  p256_ref, p128_ref, bucket_ref, sinks_ref, relb_ref,
                  o_ref,
                  ubuf, cbuf, hcar, kbuf, vbuf, st, ycat, biasm, g256, tri):
    b_idx = pl.program_id(0)
    t_idx = pl.program_id(1)

    @pl.when((b_idx == 0) & (t_idx == 0))
    def _build_tables():
        ri = lax.broadcasted_iota(jnp.int32, (W, W), 0) // HEAD
        ci = lax.broadcasted_iota(jnp.int32, (W, W), 1) // HEAD
        g256[...] = jnp.where(ri == ci, 1.0 / HEAD, 0.0).astype(BF16)
        rt = lax.broadcasted_iota(jnp.int32, (TB, TB), 0)
        ct = lax.broadcasted_iota(jnp.int32, (TB, TB), 1)
        tri[...] = jnp.where((rt // CHUNK == ct // CHUNK) & (rt >= ct), 1.0, 0.0).astype(BF16)
        bucket = bucket_ref[...]
        qi = lax.broadcasted_iota(jnp.int32, (QBLK, 2 * QBLK), 0)
        kj = lax.broadcasted_iota(jnp.int32, (QBLK, 2 * QBLK), 1)
        dist = qi + QBLK - kj
        valid = (dist >= 0) & (dist < WINDOW)
        for h in range(4):
            acc = jnp.zeros((QBLK, 2 * QBLK), F32)
            for bk in range(N_BUCKETS):
                acc = jnp.where(bucket == bk, relb_ref[bk * 4 + h], acc)
            biasm[h * QBLK:(h + 1) * QBLK, :] = jnp.where(valid, acc, -jnp.inf)

    @pl.when(t_idx == 0)
    def _reset_state():
        ubuf[0:A_PAD, :] = jnp.zeros((A_PAD, W), F32)
        cbuf[0:C_PAD, :] = jnp.zeros((C_PAD, W), F32)
        hcar[...] = jnp.zeros((SUBLANES, W), F32)
        kbuf[0:QBLK, :] = jnp.zeros((QBLK, LANES), BF16)
        vbuf[0:QBLK, :] = jnp.zeros((QBLK, LANES), BF16)
        st[...] = jnp.zeros((W, LANES), F32)

    x = x_ref[0]
    ms = jnp.mean(x * x, axis=-1, keepdims=True)
    hb = (x * lax.rsqrt(ms + EPS) * ng_ref[...]).astype(BF16)

    gmat = g256[...]

    pa = _dot(hb, win_ref[:, A_OFF:A_OFF + A_WID])
    u = pa[:, 0:W] * _sigmoid(pa[:, W:2 * W])
    ubuf[A_PAD:A_PAD + TB, :] = u
    conv = jnp.broadcast_to(p256_ref[R_ACB:R_ACB + 1, :], (TB, W))
    conv = _causal_taps(conv, ubuf[...], aconv_ref, 0, CONV_A, A_PAD)
    ubuf[0:A_PAD, :] = ubuf[TB:TB + A_PAD, :]
    mu = _dot_split(conv, gmat, 2)
    dc = conv - mu
    var = _dot_split(dc * dc, gmat, 2)
    un = dc * lax.rsqrt(var + EPS) * p256_ref[R_ALG:R_ALG + 1, :] + p256_ref[R_ALB:R_ALB + 1, :]
    ya = _dot(_silu(un).astype(BF16), apw_ref[...]) * _silu(pa[:, 2 * W:3 * W])
    ycat[:, 0:W] = ya.astype(BF16)

    pb = _dot(hb, win_ref[:, B_OFF:B_OFF + B_WID])
    q = pb[:, 0:W]
    k = pb[:, W:W + LANES]
    v = pb[:, W + LANES:W + 2 * LANES]
    qn = q * lax.rsqrt(_dot_split(q * q, gmat, 2) + EPS) * p256_ref[R_BQG:R_BQG + 1, :]
    kn = k * lax.rsqrt(_dot_split(k * k, gmat[0:LANES, 0:LANES], 2) + EPS) * p128_ref[0:1, :]
    kbuf[QBLK:QBLK + TB, :] = kn.astype(BF16)
    vbuf[QBLK:QBLK + TB, :] = v.astype(BF16)
    lane = lax.broadcasted_iota(jnp.int32, (1, LANES), 1)
    lo = (lane < HEAD).astype(F32)
    hi = 1.0 - lo
    first_tile_mask = jnp.where(t_idx == 0, -jnp.inf, 0.0)
    scale = HEAD ** -0.5
    for bi in range(TB // QBLK):
        r0 = bi * QBLK
        qa = qn[r0:r0 + QBLK, 0:LANES]
        qb = qn[r0:r0 + QBLK, LANES:2 * LANES]
        qst = jnp.concatenate([qa * lo, qb * lo, qa * hi, qb * hi], axis=0).astype(BF16)
        kb = kbuf[r0:r0 + 2 * QBLK, :]
        vb = vbuf[r0:r0 + 2 * QBLK, :]
        s = _dot_nt(qst, kb) * scale + biasm[...]
        outs = []
        for h in range(4):
            sh = s[h * QBLK:(h + 1) * QBLK, :]
            if bi == 0:
                sh = jnp.concatenate([sh[:, 0:QBLK] + first_tile_mask, sh[:, QBLK:]], axis=1)
            sink = sinks_ref[h]
            m = jnp.maximum(jnp.max(sh, axis=-1, keepdims=True), sink)
            p = jnp.exp(sh - m)
            den = jnp.sum(p, axis=-1, keepdims=True) + jnp.exp(sink - m)
            outs.append((p.astype(BF16), den))
        pst = jnp.concatenate([o[0] for o in outs], axis=0)
        ov = _dot(pst, vb)
        oh = [ov[h * QBLK:(h + 1) * QBLK, :] / outs[h][1] for h in range(4)]
        lanem = lane < HEAD
        yb_blk = jnp.concatenate([jnp.where(lanem, oh[0], oh[2]), jnp.where(lanem, oh[1], oh[3])], axis=1)
        gate = pb[r0:r0 + QBLK, W + 2 * LANES:2 * W + 2 * LANES]
        ycat[r0:r0 + QBLK, W:2 * W] = (yb_blk * _silu(gate)).astype(BF16)
    kbuf[0:QBLK, :] = kbuf[TB:TB + QBLK, :]
    vbuf[0:QBLK, :] = vbuf[TB:TB + QBLK, :]

    pc = _dot(hb, win_ref[:, C_OFF:C_OFF + C_WID])
    cbuf[C_PAD:C_PAD + TB, :] = pc[:, 0:W]
    xc = jnp.broadcast_to(p256_ref[R_CCB:R_CCB + 1, :], (TB, W))
    xc = _causal_taps(xc, cbuf[...], p256_ref, R_CCW, CONV_C, C_PAD)
    cbuf[0:C_PAD, :] = cbuf[TB:TB + C_PAD, :]
    xcb = xc.astype(BF16)
    rg = _sigmoid(_dot(xcb, cwr_ref[...]) + p256_ref[R_CBR:R_CBR + 1, :])
    ig = _sigmoid(_dot(xcb, cwi_ref[...]) + p256_ref[R_CBI:R_CBI + 1, :])
    nlam = -p256_ref[R_LAM:R_LAM + 1, :]
    softplus = jnp.maximum(nlam, 0.0) + jnp.log1p(jnp.exp(-jnp.abs(nlam)))
    log_a = (-LRU_C * rg) * softplus
    a = jnp.exp(log_a)
    uu = jnp.sqrt(1.0 - jnp.exp(2.0 * log_a)) * (ig * xc)
    step = 1
    while step < TB:
        uu = a * _shift_rows(uu, step, 0.0) + uu
        a = a * _shift_rows(a, step, 1.0)
        step *= 2
    hs = uu + a * hcar[0:1, :]
    hcar[0:1, :] = hs[TB - 1:TB, :]
    ycat[:, 2 * W:3 * W] = (hs * _silu(pc[:, W:2 * W])).astype(BF16)

    pd = _dot(hb, win_ref[:, D_OFF:D_OFF + D_WID])
    dq = pd[:, 0:LANES] * (D_KEY ** -0.5)
    dk = pd[:, LANES:2 * LANES]
    dv = pd[:, 2 * LANES:2 * LANES + W]
    dgate = pd[:, 2 * LANES + W:2 * LANES + 2 * W]
    dlr = pd[:, 2 * LANES + 2 * W:3 * LANES + 2 * W]
    z = _dot(dlr.astype(BF16), dwup_ref[...]) + p128_ref[1:2, :]
    lg = (jnp.minimum(z, 0.0) - jnp.log1p(jnp.exp(-jnp.abs(z)))) * (1.0 / GATE_TAU)
    bcum = None
    r = lg
    for p_ in range(3):
        part = r.astype(BF16)
        d_ = _dot(tri[...], part)
        bcum = d_ if bcum is None else bcum + d_
        r = r - part.astype(F32)
    lane256 = lax.broadcasted_iota(jnp.int32, (1, W), 1)
    ci = lax.broadcasted_iota(jnp.int32, (CHUNK, W), 0)
    cj = lax.broadcasted_iota(jnp.int32, (CHUNK, W), 1) % CHUNK
    causal = (ci >= cj).astype(F32)
    bdm = (lax.broadcasted_iota(jnp.int32, (W, LANES), 0) // HEAD
           == lax.broadcasted_iota(jnp.int32, (W, LANES), 1) // D_KEY).astype(F32)
    state = st[...]
    od_parts = []
    for c in range(TB // CHUNK):
        r0 = c * CHUNK
        bc = bcum[r0:r0 + CHUNK, :]
        bl = bc[CHUNK - 1:CHUNK, :]
        qt = (dq[r0:r0 + CHUNK, :] * jnp.exp(bc)).astype(BF16)
        kc = dk[r0:r0 + CHUNK, :]
        kt = kc * jnp.exp(-bc)
        ke = (kc * jnp.exp(bl - bc)).astype(BF16)
        vc = dv[r0:r0 + CHUNK, :]
        kst = jnp.concatenate([kt * (lane // D_KEY == h).astype(F32) for h in range(D_HEADS)],
                              axis=0).astype(BF16)
        att = (_dot_nt(qt, kst) * causal).astype(BF16)
        vbd = jnp.concatenate([vc * (lane256 // HEAD == h).astype(F32) for h in range(D_HEADS)],
                              axis=0).astype(BF16)
        o_c = _dot(att, vbd) + _dot_nt(qt, state.astype(BF16))
        state = state * jnp.exp(bl) + _dot_tn(vc.astype(BF16), ke) * bdm
        od_parts.append(o_c)
    st[...] = state
    od = jnp.concatenate(od_parts, axis=0)
    odn = od * lax.rsqrt(_dot_split(od * od, gmat, 2) + EPS) * p256_ref[R_DNG:R_DNG + 1, :]
    ycat[:, 3 * W:4 * W] = (odn * _silu(dgate)).astype(BF16)

    o_ref[0] = x + _dot(ycat[...], wout_ref[...])


def _t5_bucket(dist):
    max_exact = N_BUCKETS // 2
    d = jnp.maximum(dist, 1).astype(F32)
    large = max_exact + (jnp.log(d / max_exact) / math.log(MAX_DISTANCE / max_exact)
                         * (N_BUCKETS - max_exact)).astype(jnp.int32)
    large = jnp.minimum(large, N_BUCKETS - 1)
    return jnp.where(dist < max_exact, dist, large)


def _full(shape):
    return pl.BlockSpec(shape, lambda b, t: (0,) * len(shape))


def _layer_call(x, ng, win, wout, apw, cwr, cwi, dwup, aconv, p256, p128, bucket, sinks, relb):
    bsz, seq, _ = x.shape
    smem = pl.BlockSpec(memory_space=pltpu.SMEM)
    return pl.pallas_call(
        _layer_kernel,
        grid=(bsz, seq // TB),
        in_specs=[
            pl.BlockSpec((1, TB, D_MODEL), lambda b, t: (b, t, 0)),
            _full((1, D_MODEL)), _full((D_MODEL, IN_PAD)), _full((D_MODEL, D_MODEL)),
            _full((W, W)), _full((W, W)), _full((W, W)), _full((LANES, LANES)),
            _full((A_PAD, W)), _full((16, W)), _full((SUBLANES, LANES)), _full((QBLK, 2 * QBLK)),
            smem, smem,
        ],
        out_specs=pl.BlockSpec((1, TB, D_MODEL), lambda b, t: (b, t, 0)),
        out_shape=jax.ShapeDtypeStruct(x.shape, F32),
        scratch_shapes=[
            pltpu.VMEM((TB + A_PAD, W), F32),
            pltpu.VMEM((TB + C_PAD, W), F32),
            pltpu.VMEM((SUBLANES, W), F32),
            pltpu.VMEM((TB + QBLK, LANES), BF16),
            pltpu.VMEM((TB + QBLK, LANES), BF16),
            pltpu.VMEM((W, LANES), F32),
            pltpu.VMEM((TB, D_MODEL), BF16),
            pltpu.VMEM((4 * QBLK, 2 * QBLK), F32),
            pltpu.VMEM((W, W), BF16),
            pltpu.VMEM((TB, TB), BF16),
        ],
        compiler_params=pltpu.CompilerParams(
            dimension_semantics=("arbitrary", "arbitrary"),
            vmem_limit_bytes=48 * 1024 * 1024,
        ),
        name="hybrid_layer",
    )(x, ng, win, wout, apw, cwr, cwi, dwup, aconv, p256, p128, bucket, sinks, relb)


def _block_diag(blocks):
    n, c, _ = blocks.shape
    eye = jnp.eye(n, dtype=blocks.dtype)
    return (eye[:, None, :, None] * blocks[:, :, None, :]).reshape(n * c, n * c)


def _head_perm(w, axis):
    parts = jnp.split(w, 4, axis=axis)
    return jnp.concatenate([parts[0], parts[2], parts[1], parts[3]], axis=axis)


def kernel(x, norm_g, w_in, a_conv_w, a_conv_b, a_ln_g, a_ln_b, a_pw, b_q_g, b_k_g, b_sinks, rel_bias,
           c_conv_w, c_conv_b, c_w_r, c_b_r, c_w_i, c_b_i, c_lambda, d_w_up, d_b_up, d_norm_g, w_out):
    dist = jnp.arange(QBLK)[:, None] + QBLK - jnp.arange(2 * QBLK)[None, :]
    bucket = _t5_bucket(jnp.clip(dist, 0, None)).astype(jnp.int32)
    relb = rel_bias.reshape(-1)
    for l in range(DEPTH):
        wi = w_in[l]
        win = jnp.concatenate([
            wi[:, 0:768],
            _head_perm(wi[:, 768:1024], 1), wi[:, 1024:1280], _head_perm(wi[:, 1280:1536], 1),
            wi[:, 1536:2048],
            wi[:, 2048:2560], wi[:, 2576:2832], wi[:, 2560:2576],
            jnp.zeros((D_MODEL, IN_PAD - 2832), F32),
        ], axis=1).astype(BF16)
        wo = w_out[l]
        wout = jnp.concatenate([wo[0:256], _head_perm(wo[256:512], 0), wo[512:1024]], axis=0).astype(BF16)
        dwup = jnp.concatenate([d_w_up[l], jnp.zeros((LANES - GATE_RANK, LANES), F32)], axis=0).astype(BF16)
        aconv = jnp.concatenate([a_conv_w[l], jnp.zeros((A_PAD - CONV_A, W), F32)], axis=0)
        p256 = jnp.concatenate([
            a_conv_b[l][None], a_ln_g[l][None], a_ln_b[l][None], c_conv_b[l][None], c_b_r[l][None],
            c_b_i[l][None], c_lambda[l][None], jnp.tile(d_norm_g[l], 4)[None], c_conv_w[l],
            jnp.tile(b_q_g[l], 4)[None], jnp.zeros((3, W), F32)], axis=0)
        p128 = jnp.concatenate([jnp.tile(b_k_g[l], 2)[None], d_b_up[l][None],
                                jnp.zeros((SUBLANES - 2, LANES), F32)], axis=0)
        x = _layer_call(x, norm_g[l][None], win, wout, a_pw[l].astype(BF16),
                        _block_diag(c_w_r[l]).astype(BF16), _block_diag(c_w_i[l]).astype(BF16),
                        dwup, aconv, p256, p128, bucket, b_sinks[l], relb)
    return x
```

```python
import functools
import math

import jax
import jax.numpy as jnp
from jax import lax
from jax.experimental import pallas as pl
from jax.experimental.pallas import tpu as pltpu

F32 = jnp.float32
BF16 = jnp.bfloat16

D_MODEL = 1024
DEPTH = 4
W = 256
EPS = 1e-6
CONV_A = 31
A_PAD = 32
HEAD = 64
WINDOW = 128
QBLK = 128
N_BUCKETS = 32
MAX_DISTANCE = 128
CONV_C = 4
C_PAD = 8
LRU_C = 8.0
D_HEADS = 4
D_KEY = 32
GATE_RANK = 16
GATE_TAU = 16.0
CHUNK = 64
SUBLANES = 8
LANES = 128

TB = 256

A_OFF, A_WID = 0, 768
B_OFF, B_WID = 768, 768
C_OFF, C_WID = 1536, 512
D_OFF, D_WID = 2048, 896
IN_PAD = D_OFF + D_WID

R_ACB, R_ALG, R_ALB, R_CCB, R_CBR, R_CBI, R_LAM, R_DNG, R_CCW, R_BQG = 0, 1, 2, 3, 4, 5, 6, 7, 8, 12


def _dot(a, b):
    return jnp.dot(a, b, preferred_element_type=F32)


def _dot_nt(a, b):
    return lax.dot_general(a, b, (((1,), (1,)), ((), ())), preferred_element_type=F32)


def _dot_tn(a, b):
    return lax.dot_general(a, b, (((0,), (0,)), ((), ())), preferred_element_type=F32)


def _dot_split(x, w_bf, passes):
    acc = None
    r = x
    for p in range(passes):
        part = r.astype(BF16)
        d = _dot(part, w_bf)
        acc = d if acc is None else acc + d
        if p + 1 < passes:
            r = r - part.astype(F32)
    return acc


def _sigmoid(x):
    return 1.0 / (1.0 + jnp.exp(-x))


def _silu(x):
    return x * _sigmoid(x)


def _causal_taps(acc, buf, w_ref, w_row0, n_taps, pad):
    rows = buf.shape[0]
    for res in range(SUBLANES):
        shifted = None
        for j in range(n_taps):
            off = pad - (n_taps - 1) + j
            if off % SUBLANES != res:
                continue
            if shifted is None:
                shifted = buf if res == 0 else pltpu.roll(buf, rows - res, axis=0)
            base = off - res
            acc = acc + w_ref[w_row0 + j:w_row0 + j + 1, :] * shifted[base:base + TB]
    return acc


def _shift_rows(x, step, fill):
    if step % SUBLANES == 0:
        return jnp.concatenate([jnp.full((step, x.shape[1]), fill, F32), x[0:x.shape[0] - step]], axis=0)
    rolled = pltpu.roll(x, step, axis=0)
    row = lax.broadcasted_iota(jnp.int32, (SUBLANES, x.shape[1]), 0)
    head = jnp.where(row < step, fill, rolled[0:SUBLANES])
    return jnp.concatenate([head, rolled[SUBLANES:]], axis=0)


def _layer_kernel(x_ref, ng_ref, win_ref, wout_ref, apw_ref, cwr_ref, cwi_ref, dwup_ref, aconv_ref,
                  p256_ref, p128_ref, bucket_ref, sinks_ref, relb_ref,
                  o_ref,
                  ubuf, cbuf, hcar, kbuf, vbuf, st, ycat, biasm, g256, tri, pa, pb, pc, pd):
    b_idx = pl.program_id(0)
    t_idx = pl.program_id(1)

    @pl.when((b_idx == 0) & (t_idx == 0))
    def _build_tables():
        ri = lax.broadcasted_iota(jnp.int32, (W, W), 0) // HEAD
        ci = lax.broadcasted_iota(jnp.int32, (W, W), 1) // HEAD
        g256[...] = jnp.where(ri == ci, 1.0 / HEAD, 0.0).astype(BF16)
        rt = lax.broadcasted_iota(jnp.int32, (TB, TB), 0)
        ct = lax.broadcasted_iota(jnp.int32, (TB, TB), 1)
        tri[...] = jnp.where((rt // CHUNK == ct // CHUNK) & (rt >= ct), 1.0, 0.0).astype(BF16)
        bucket = bucket_ref[...]
        qi = lax.broadcasted_iota(jnp.int32, (QBLK, 2 * QBLK), 0)
        kj = lax.broadcasted_iota(jnp.int32, (QBLK, 2 * QBLK), 1)
        dist = qi + QBLK - kj
        valid = (dist >= 0) & (dist < WINDOW)
        for h in range(4):
            acc = jnp.zeros((QBLK, 2 * QBLK), F32)
            for bk in range(N_BUCKETS):
                acc = jnp.where(bucket == bk, relb_ref[bk * 4 + h], acc)
            biasm[h * QBLK:(h + 1) * QBLK, :] = jnp.where(valid, acc, -jnp.inf)

    @pl.when(t_idx == 0)
    def _reset_state():
        ubuf[0:A_PAD, :] = jnp.zeros((A_PAD, W), F32)
        cbuf[0:C_PAD, :] = jnp.zeros((C_PAD, W), F32)
        hcar[...] = jnp.zeros((SUBLANES, W), F32)
        kbuf[0:QBLK, :] = jnp.zeros((QBLK, LANES), BF16)
        vbuf[0:QBLK, :] = jnp.zeros((QBLK, LANES), BF16)
        st[...] = jnp.zeros((W, LANES), F32)

    x = x_ref[0]
    ms = jnp.mean(x * x, axis=-1, keepdims=True)
    hb = (x * lax.rsqrt(ms + EPS) * ng_ref[...]).astype(BF16)

    gmat = g256[...]

    pa[...] = _dot(hb, win_ref[:, A_OFF:A_OFF + A_WID])
    pb[...] = _dot(hb, win_ref[:, B_OFF:B_OFF + B_WID])
    pc[...] = _dot(hb, win_ref[:, C_OFF:C_OFF + C_WID])
    pd[...] = _dot(hb, win_ref[:, D_OFF:D_OFF + D_WID])
    u = pa[:, 0:W] * _sigmoid(pa[:, W:2 * W])
    ubuf[A_PAD:A_PAD + TB, :] = u
    conv = jnp.broadcast_to(p256_ref[R_ACB:R_ACB + 1, :], (TB, W))
    conv = _causal_taps(conv, ubuf[...], aconv_ref, 0, CONV_A, A_PAD)
    ubuf[0:A_PAD, :] = ubuf[TB:TB + A_PAD, :]
    mu = _dot_split(conv, gmat, 2)
    dc = conv - mu
    var = _dot_split(dc * dc, gmat, 1)
    un = dc * lax.rsqrt(var + EPS) * p256_ref[R_ALG:R_ALG + 1, :] + p256_ref[R_ALB:R_ALB + 1, :]
    ya = _dot(_silu(un).astype(BF16), apw_ref[...]) * _silu(pa[:, 2 * W:3 * W])
    ycat[:, 0:W] = ya.astype(BF16)

    q = pb[:, 0:W]
    k = pb[:, W:W + LANES]
    v = pb[:, W + LANES:W + 2 * LANES]
    qn = q * lax.rsqrt(_dot_split(q * q, gmat, 1) + EPS) * p256_ref[R_BQG:R_BQG + 1, :]
    kn = k * lax.rsqrt(_dot_split(k * k, gmat[0:LANES, 0:LANES], 1) + EPS) * p128_ref[0:1, :]
    kbuf[QBLK:QBLK + TB, :] = kn.astype(BF16)
    vbuf[QBLK:QBLK + TB, :] = v.astype(BF16)
    lane = lax.broadcasted_iota(jnp.int32, (1, LANES), 1)
    scale = HEAD ** -0.5
    lo = jnp.where(lane < HEAD, scale, 0.0)
    hi = jnp.where(lane < HEAD, 0.0, scale)
    first_tile_mask = jnp.where(t_idx == 0, -jnp.inf, 0.0)
    for bi in range(TB // QBLK):
        r0 = bi * QBLK
        qa = qn[r0:r0 + QBLK, 0:LANES]
        qb = qn[r0:r0 + QBLK, LANES:2 * LANES]
        qst = jnp.concatenate([qa * lo, qb * lo, qa * hi, qb * hi], axis=0).astype(BF16)
        kb = kbuf[r0:r0 + 2 * QBLK, :]
        vb = vbuf[r0:r0 + 2 * QBLK, :]
        s = _dot_nt(qst, kb) + biasm[...]
        outs = []
        for h in range(4):
            sh = s[h * QBLK:(h + 1) * QBLK, :]
            if bi == 0:
                sh = jnp.concatenate([sh[:, 0:QBLK] + first_tile_mask, sh[:, QBLK:]], axis=1)
            sink = sinks_ref[h]
            m = jnp.maximum(jnp.max(sh, axis=-1, keepdims=True), sink)
            p = jnp.exp(sh - m)
            den = jnp.sum(p, axis=-1, keepdims=True) + jnp.exp(sink - m)
            outs.append((p.astype(BF16), den))
        pst = jnp.concatenate([o[0] for o in outs], axis=0)
        ov = _dot(pst, vb)
        oh = [ov[h * QBLK:(h + 1) * QBLK, :] / outs[h][1] for h in range(4)]
        lanem = lane < HEAD
        yb_blk = jnp.concatenate([jnp.where(lanem, oh[0], oh[2]), jnp.where(lanem, oh[1], oh[3])], axis=1)
        gate = pb[r0:r0 + QBLK, W + 2 * LANES:2 * W + 2 * LANES]
        ycat[r0:r0 + QBLK, W:2 * W] = (yb_blk * _silu(gate)).astype(BF16)
    kbuf[0:QBLK, :] = kbuf[TB:TB + QBLK, :]
    vbuf[0:QBLK, :] = vbuf[TB:TB + QBLK, :]

    cbuf[C_PAD:C_PAD + TB, :] = pc[:, 0:W]
    xc = jnp.broadcast_to(p256_ref[R_CCB:R_CCB + 1, :], (TB, W))
    xc = _causal_taps(xc, cbuf[...], p256_ref, R_CCW, CONV_C, C_PAD)
    cbuf[0:C_PAD, :] = cbuf[TB:TB + C_PAD, :]
    xcb = xc.astype(BF16)
    rg = _sigmoid(_dot(xcb, cwr_ref[...]) + p256_ref[R_CBR:R_CBR + 1, :])
    ig = _sigmoid(_dot(xcb, cwi_ref[...]) + p256_ref[R_CBI:R_CBI + 1, :])
    nlam = -p256_ref[R_LAM:R_LAM + 1, :]
    softplus = jnp.maximum(nlam, 0.0) + jnp.log1p(jnp.exp(-jnp.abs(nlam)))
    log_a = (-LRU_C * rg) * softplus
    a = jnp.exp(log_a)
    uu = jnp.sqrt(1.0 - jnp.exp(2.0 * log_a)) * (ig * xc)
    step = 1
    while step < TB:
        uu = a * _shift_rows(uu, step, 0.0) + uu
        a = a * _shift_rows(a, step, 1.0)
        step *= 2
    hs = uu + a * hcar[0:1, :]
    hcar[0:1, :] = hs[TB - 1:TB, :]
    ycat[:, 2 * W:3 * W] = (hs * _silu(pc[:, W:2 * W])).astype(BF16)

    dq = pd[:, 0:LANES] * (D_KEY ** -0.5)
    dk = pd[:, LANES:2 * LANES]
    dv = pd[:, 2 * LANES:2 * LANES + W]
    dgate = pd[:, 2 * LANES + W:2 * LANES + 2 * W]
    dlr = pd[:, 2 * LANES + 2 * W:3 * LANES + 2 * W]
    z = _dot(dlr.astype(BF16), dwup_ref[...]) + p128_ref[1:2, :]
    lg = (jnp.minimum(z, 0.0) - jnp.log1p(jnp.exp(-jnp.abs(z)))) * (1.0 / GATE_TAU)
    bcum = None
    r = lg
    for p_ in range(2):
        part = r.astype(BF16)
        d_ = _dot(tri[...], part)
        bcum = d_ if bcum is None else bcum + d_
        r = r - part.astype(F32)
    lane256 = lax.broadcasted_iota(jnp.int32, (1, W), 1)
    ci = lax.broadcasted_iota(jnp.int32, (CHUNK, W), 0)
    cj = lax.broadcasted_iota(jnp.int32, (CHUNK, W), 1) % CHUNK
    causal = (ci >= cj).astype(F32)
    bdm = (lax.broadcasted_iota(jnp.int32, (W, LANES), 0) // HEAD
           == lax.broadcasted_iota(jnp.int32, (W, LANES), 1) // D_KEY).astype(F32)
    state = st[...]
    od_parts = []
    for c in range(TB // CHUNK):
        r0 = c * CHUNK
        bc = bcum[r0:r0 + CHUNK, :]
        bl = bc[CHUNK - 1:CHUNK, :]
        qt = (dq[r0:r0 + CHUNK, :] * jnp.exp(bc)).astype(BF16)
        kc = dk[r0:r0 + CHUNK, :]
        kt = kc * jnp.exp(-bc)
        ke = (kc * jnp.exp(bl - bc)).astype(BF16)
        vc = dv[r0:r0 + CHUNK, :]
        kst = jnp.concatenate([kt * (lane // D_KEY == h).astype(F32) for h in range(D_HEADS)],
                              axis=0).astype(BF16)
        att = (_dot_nt(qt, kst) * causal).astype(BF16)
        vbd = jnp.concatenate([vc * (lane256 // HEAD == h).astype(F32) for h in range(D_HEADS)],
                              axis=0).astype(BF16)
        o_c = _dot(att, vbd) + _dot_nt(qt, state.astype(BF16))
        state = state * jnp.exp(bl) + _dot_tn(vc.astype(BF16), ke) * bdm
        od_parts.append(o_c)
    st[...] = state
    od = jnp.concatenate(od_parts, axis=0)
    odn = od * lax.rsqrt(_dot_split(od * od, gmat, 1) + EPS) * p256_ref[R_DNG:R_DNG + 1, :]
    ycat[:, 3 * W:4 * W] = (odn * _silu(dgate)).astype(BF16)

    o_ref[0] = x + _dot(ycat[...], wout_ref[...])


def _t5_bucket(dist):
    max_exact = N_BUCKETS // 2
    d = jnp.maximum(dist, 1).astype(F32)
    large = max_exact + (jnp.log(d / max_exact) / math.log(MAX_DISTANCE / max_exact)
                         * (N_BUCKETS - max_exact)).astype(jnp.int32)
    large = jnp.minimum(large, N_BUCKETS - 1)
    return jnp.where(dist < max_exact, dist, large)


def _full(shape):
    return pl.BlockSpec(shape, lambda b, t: (0,) * len(shape))


def _layer_call(x, ng, win, wout, apw, cwr, cwi, dwup, aconv, p256, p128, bucket, sinks, relb):
    bsz, seq, _ = x.shape
    smem = pl.BlockSpec(memory_space=pltpu.SMEM)
    return pl.pallas_call(
        _layer_kernel,
        grid=(bsz, seq // TB),
        in_specs=[
            pl.BlockSpec((1, TB, D_MODEL), lambda b, t: (b, t, 0)),
            _full((1, D_MODEL)), _full((D_MODEL, IN_PAD)), _full((D_MODEL, D_MODEL)),
            _full((W, W)), _full((W, W)), _full((W, W)), _full((LANES, LANES)),
            _full((A_PAD, W)), _full((16, W)), _full((SUBLANES, LANES)), _full((QBLK, 2 * QBLK)),
            smem, smem,
        ],
        out_specs=pl.BlockSpec((1, TB, D_MODEL), lambda b, t: (b, t, 0)),
        out_shape=jax.ShapeDtypeStruct(x.shape, F32),
        scratch_shapes=[
            pltpu.VMEM((TB + A_PAD, W), F32),
            pltpu.VMEM((TB + C_PAD, W), F32),
            pltpu.VMEM((SUBLANES, W), F32),
            pltpu.VMEM((TB + QBLK, LANES), BF16),
            pltpu.VMEM((TB + QBLK, LANES), BF16),
            pltpu.VMEM((W, LANES), F32),
            pltpu.VMEM((TB, D_MODEL), BF16),
            pltpu.VMEM((4 * QBLK, 2 * QBLK), F32),
            pltpu.VMEM((W, W), BF16),
            pltpu.VMEM((TB, TB), BF16),
            pltpu.VMEM((TB, A_WID), F32),
            pltpu.VMEM((TB, B_WID), F32),
            pltpu.VMEM((TB, C_WID), F32),
            pltpu.VMEM((TB, D_WID), F32),
        ],
        compiler_params=pltpu.CompilerParams(
            dimension_semantics=("arbitrary", "arbitrary"),
            vmem_limit_bytes=48 * 1024 * 1024,
        ),
        name="hybrid_layer",
    )(x, ng, win, wout, apw, cwr, cwi, dwup, aconv, p256, p128, bucket, sinks, relb)


def _block_diag(blocks):
    n, c, _ = blocks.shape
    eye = jnp.eye(n, dtype=blocks.dtype)
    return (eye[:, None, :, None] * blocks[:, :, None, :]).reshape(n * c, n * c)


def _head_perm(w, axis):
    parts = jnp.split(w, 4, axis=axis)
    return jnp.concatenate([parts[0], parts[2], parts[1], parts[3]], axis=axis)


def kernel(x, norm_g, w_in, a_conv_w, a_conv_b, a_ln_g, a_ln_b, a_pw, b_q_g, b_k_g, b_sinks, rel_bias,
           c_conv_w, c_conv_b, c_w_r, c_b_r, c_w_i, c_b_i, c_lambda, d_w_up, d_b_up, d_norm_g, w_out):
    dist = jnp.arange(QBLK)[:, None] + QBLK - jnp.arange(2 * QBLK)[None, :]
    bucket = _t5_bucket(jnp.clip(dist, 0, None)).astype(jnp.int32)
    relb = rel_bias.reshape(-1)
    for l in range(DEPTH):
        wi = w_in[l]
        win = jnp.concatenate([
            wi[:, 0:768],
            _head_perm(wi[:, 768:1024], 1), wi[:, 1024:1280], _head_perm(wi[:, 1280:1536], 1),
            wi[:, 1536:2048],
            wi[:, 2048:2560], wi[:, 2576:2832], wi[:, 2560:2576],
            jnp.zeros((D_MODEL, IN_PAD - 2832), F32),
        ], axis=1).astype(BF16)
        wo = w_out[l]
        wout = jnp.concatenate([wo[0:256], _head_perm(wo[256:512], 0), wo[512:1024]], axis=0).astype(BF16)
        dwup = jnp.concatenate([d_w_up[l], jnp.zeros((LANES - GATE_RANK, LANES), F32)], axis=0).astype(BF16)
        aconv = jnp.concatenate([a_conv_w[l], jnp.zeros((A_PAD - CONV_A, W), F32)], axis=0)
        p256 = jnp.concatenate([
            a_conv_b[l][None], a_ln_g[l][None], a_ln_b[l][None], c_conv_b[l][None], c_b_r[l][None],
            c_b_i[l][None], c_lambda[l][None], jnp.tile(d_norm_g[l], 4)[None], c_conv_w[l],
            jnp.tile(b_q_g[l], 4)[None], jnp.zeros((3, W), F32)], axis=0)
        p128 = jnp.concatenate([jnp.tile(b_k_g[l], 2)[None], d_b_up[l][None],
                                jnp.zeros((SUBLANES - 2, LANES), F32)], axis=0)
        x = _layer_call(x, norm_g[l][None], win, wout, a_pw[l].astype(BF16),
                        _block_diag(c_w_r[l]).astype(BF16), _block_diag(c_w_i[l]).astype(BF16),
                        dwup, aconv, p256, p128, bucket, b_sinks[l], relb)
    return x
```

```python
import functools
import math

import jax
import jax.numpy as jnp
from jax import lax
from jax.experimental import pallas as pl
from jax.experimental.pallas import tpu as pltpu

F32 = jnp.float32
BF16 = jnp.bfloat16

D_MODEL = 1024
DEPTH = 4
W = 256
EPS = 1e-6
CONV_A = 31
A_PAD = 32
HEAD = 64
WINDOW = 128
QBLK = 128
N_BUCKETS = 32
MAX_DISTANCE = 128
CONV_C = 4
C_PAD = 8
LRU_C = 8.0
D_HEADS = 4
D_KEY = 32
GATE_RANK = 16
GATE_TAU = 16.0
CHUNK = 64
SUBLANES = 8
LANES = 128

TB = 256

A_OFF, A_WID = 0, 768
B_OFF, B_WID = 768, 768
C_OFF, C_WID = 1536, 512
D_OFF, D_WID = 2048, 896
IN_PAD = D_OFF + D_WID

R_ACB, R_ALG, R_ALB, R_CCB, R_CBR, R_CBI, R_LAM, R_DNG, R_CCW, R_BQG = 0, 1, 2, 3, 4, 5, 6, 7, 8, 12


def _dot(a, b):
    return jnp.dot(a, b, preferred_element_type=F32)


def _dot_nt(a, b):
    return lax.dot_general(a, b, (((1,), (1,)), ((), ())), preferred_element_type=F32)


def _dot_tn(a, b):
    return lax.dot_general(a, b, (((0,), (0,)), ((), ())), preferred_element_type=F32)


def _dot_split(x, w_bf, passes):
    acc = None
    r = x
    for p in range(passes):
        part = r.astype(BF16)
        d = _dot(part, w_bf)
        acc = d if acc is None else acc + d
        if p + 1 < passes:
            r = r - part.astype(F32)
    return acc


def _sigmoid(x):
    return 1.0 / (1.0 + jnp.exp(-x))


def _silu(x):
    return x * _sigmoid(x)


def _causal_taps(acc, buf, w_ref, w_row0, n_taps, pad):
    rows = buf.shape[0]
    for res in range(SUBLANES):
        shifted = None
        for j in range(n_taps):
            off = pad - (n_taps - 1) + j
            if off % SUBLANES != res:
                continue
            if shifted is None:
                shifted = buf if res == 0 else pltpu.roll(buf, rows - res, axis=0)
            base = off - res
            acc = acc + w_ref[w_row0 + j:w_row0 + j + 1, :] * shifted[base:base + TB]
    return acc


def _shift_rows(x, step, fill):
    if step % SUBLANES == 0:
        return jnp.concatenate([jnp.full((step, x.shape[1]), fill, F32), x[0:x.shape[0] - step]], axis=0)
    rolled = pltpu.roll(x, step, axis=0)
    row = lax.broadcasted_iota(jnp.int32, (SUBLANES, x.shape[1]), 0)
    head = jnp.where(row < step, fill, rolled[0:SUBLANES])
    return jnp.concatenate([head, rolled[SUBLANES:]], axis=0)


def _layer_kernel(layer, x_ref, ng_ref, win_ref, wout_ref, apw_ref, cwr_ref, cwi_ref, dwup_ref, aconv_ref,
                  p256_ref, p128_ref, bucket_ref, sinks_ref, relb_ref,
                  o_ref,
                  ubuf, cbuf, hcar, kbuf, vbuf, st, ycat, biasm, g256, tri, pa, pb, pc, pd):
    b_idx = pl.program_id(0)
    t_idx = pl.program_id(1)

    @pl.when((b_idx == 0) & (t_idx == 0))
    def _build_tables():
        ri = lax.broadcasted_iota(jnp.int32, (W, W), 0) // HEAD
        ci = lax.broadcasted_iota(jnp.int32, (W, W), 1) // HEAD
        g256[...] = jnp.where(ri == ci, 1.0 / HEAD, 0.0).astype(BF16)
        rt = lax.broadcasted_iota(jnp.int32, (TB, TB), 0)
        ct = lax.broadcasted_iota(jnp.int32, (TB, TB), 1)
        tri[...] = jnp.where((rt // CHUNK == ct // CHUNK) & (rt >= ct), 1.0, 0.0).astype(BF16)
        bucket = bucket_ref[...]
        qi = lax.broadcasted_iota(jnp.int32, (QBLK, 2 * QBLK), 0)
        kj = lax.broadcasted_iota(jnp.int32, (QBLK, 2 * QBLK), 1)
        dist = qi + QBLK - kj
        valid = (dist >= 0) & (dist < WINDOW)
        for h in range(4):
            acc = jnp.zeros((QBLK, 2 * QBLK), F32)
            for bk in range(N_BUCKETS):
                acc = jnp.where(bucket == bk, relb_ref[bk * 4 + h], acc)
            biasm[h * QBLK:(h + 1) * QBLK, :] = jnp.where(valid, acc, -jnp.inf)

    @pl.when(t_idx == 0)
    def _reset_state():
        ubuf[0:A_PAD, :] = jnp.zeros((A_PAD, W), F32)
        cbuf[0:C_PAD, :] = jnp.zeros((C_PAD, W), F32)
        hcar[...] = jnp.zeros((SUBLANES, W), F32)
        kbuf[0:QBLK, :] = jnp.zeros((QBLK, LANES), BF16)
        vbuf[0:QBLK, :] = jnp.zeros((QBLK, LANES), BF16)
        st[...] = jnp.zeros((W, LANES), F32)

    x = x_ref[0]
    ms = jnp.mean(x * x, axis=-1, keepdims=True)
    hb = (x * lax.rsqrt(ms + EPS) * ng_ref[...]).astype(BF16)

    gmat = g256[...]

    pa[...] = _dot(hb, win_ref[:, A_OFF:A_OFF + A_WID])
    pb[...] = _dot(hb, win_ref[:, B_OFF:B_OFF + B_WID])
    pc[...] = _dot(hb, win_ref[:, C_OFF:C_OFF + C_WID])
    pd[...] = _dot(hb, win_ref[:, D_OFF:D_OFF + D_WID])
    u = pa[:, 0:W] * _sigmoid(pa[:, W:2 * W])
    ubuf[A_PAD:A_PAD + TB, :] = u
    conv = jnp.broadcast_to(p256_ref[R_ACB:R_ACB + 1, :], (TB, W))
    conv = _causal_taps(conv, ubuf[...], aconv_ref, 0, CONV_A, A_PAD)
    ubuf[0:A_PAD, :] = ubuf[TB:TB + A_PAD, :]
    mu = _dot_split(conv, gmat, 2)
    dc = conv - mu
    var = _dot_split(dc * dc, gmat, 1)
    un = dc * lax.rsqrt(var + EPS) * p256_ref[R_ALG:R_ALG + 1, :] + p256_ref[R_ALB:R_ALB + 1, :]
    ya = _dot(_silu(un).astype(BF16), apw_ref[...]) * _silu(pa[:, 2 * W:3 * W])
    ycat[:, 0:W] = ya.astype(BF16)

    q = pb[:, 0:W]
    k = pb[:, W:W + LANES]
    v = pb[:, W + LANES:W + 2 * LANES]
    qn = q * lax.rsqrt(_dot_split(q * q, gmat, 1) + EPS) * p256_ref[R_BQG:R_BQG + 1, :]
    kn = k * lax.rsqrt(_dot_split(k * k, gmat[0:LANES, 0:LANES], 1) + EPS) * p128_ref[0:1, :]
    kbuf[QBLK:QBLK + TB, :] = kn.astype(BF16)
    vbuf[QBLK:QBLK + TB, :] = v.astype(BF16)
    lane = lax.broadcasted_iota(jnp.int32, (1, LANES), 1)
    scale = HEAD ** -0.5
    lo = jnp.where(lane < HEAD, scale, 0.0)
    hi = jnp.where(lane < HEAD, 0.0, scale)
    first_tile_mask = jnp.where(t_idx == 0, -jnp.inf, 0.0)
    for bi in range(TB // QBLK):
        r0 = bi * QBLK
        qa = qn[r0:r0 + QBLK, 0:LANES]
        qb = qn[r0:r0 + QBLK, LANES:2 * LANES]
        qst = jnp.concatenate([qa * lo, qb * lo, qa * hi, qb * hi], axis=0).astype(BF16)
        kb = kbuf[r0:r0 + 2 * QBLK, :]
        vb = vbuf[r0:r0 + 2 * QBLK, :]
        s = _dot_nt(qst, kb) + biasm[...]
        outs = []
        for h in range(4):
            sh = s[h * QBLK:(h + 1) * QBLK, :]
            if bi == 0:
                sh = jnp.concatenate([sh[:, 0:QBLK] + first_tile_mask, sh[:, QBLK:]], axis=1)
            sink = sinks_ref[layer * 4 + h]
            m = jnp.maximum(jnp.max(sh, axis=-1, keepdims=True), sink)
            p = jnp.exp(sh - m)
            den = jnp.sum(p, axis=-1, keepdims=True) + jnp.exp(sink - m)
            outs.append((p.astype(BF16), den))
        pst = jnp.concatenate([o[0] for o in outs], axis=0)
        ov = _dot(pst, vb)
        oh = [ov[h * QBLK:(h + 1) * QBLK, :] / outs[h][1] for h in range(4)]
        lanem = lane < HEAD
        yb_blk = jnp.concatenate([jnp.where(lanem, oh[0], oh[2]), jnp.where(lanem, oh[1], oh[3])], axis=1)
        gate = pb[r0:r0 + QBLK, W + 2 * LANES:2 * W + 2 * LANES]
        ycat[r0:r0 + QBLK, W:2 * W] = (yb_blk * _silu(gate)).astype(BF16)
    kbuf[0:QBLK, :] = kbuf[TB:TB + QBLK, :]
    vbuf[0:QBLK, :] = vbuf[TB:TB + QBLK, :]

    cbuf[C_PAD:C_PAD + TB, :] = pc[:, 0:W]
    xc = jnp.broadcast_to(p256_ref[R_CCB:R_CCB + 1, :], (TB, W))
    xc = _causal_taps(xc, cbuf[...], p256_ref, R_CCW, CONV_C, C_PAD)
    cbuf[0:C_PAD, :] = cbuf[TB:TB + C_PAD, :]
    xcb = xc.astype(BF16)
    rg = _sigmoid(_dot(xcb, cwr_ref[...]) + p256_ref[R_CBR:R_CBR + 1, :])
    ig = _sigmoid(_dot(xcb, cwi_ref[...]) + p256_ref[R_CBI:R_CBI + 1, :])
    nlam = -p256_ref[R_LAM:R_LAM + 1, :]
    softplus = jnp.maximum(nlam, 0.0) + jnp.log1p(jnp.exp(-jnp.abs(nlam)))
    log_a = (-LRU_C * rg) * softplus
    a = jnp.exp(log_a)
    uu = jnp.sqrt(1.0 - jnp.exp(2.0 * log_a)) * (ig * xc)
    step = 1
    while step < TB:
        uu = a * _shift_rows(uu, step, 0.0) + uu
        a = a * _shift_rows(a, step, 1.0)
        step *= 2
    hs = uu + a * hcar[0:1, :]
    hcar[0:1, :] = hs[TB - 1:TB, :]
    ycat[:, 2 * W:3 * W] = (hs * _silu(pc[:, W:2 * W])).astype(BF16)

    dq = pd[:, 0:LANES] * (D_KEY ** -0.5)
    dk = pd[:, LANES:2 * LANES]
    dv = pd[:, 2 * LANES:2 * LANES + W]
    dgate = pd[:, 2 * LANES + W:2 * LANES + 2 * W]
    dlr = pd[:, 2 * LANES + 2 * W:3 * LANES + 2 * W]
    z = _dot(dlr.astype(BF16), dwup_ref[...]) + p128_ref[1:2, :]
    lg = (jnp.minimum(z, 0.0) - jnp.log1p(jnp.exp(-jnp.abs(z)))) * (1.0 / GATE_TAU)
    bcum = None
    r = lg
    for p_ in range(2):
        part = r.astype(BF16)
        d_ = _dot(tri[...], part)
        bcum = d_ if bcum is None else bcum + d_
        r = r - part.astype(F32)
    lane256 = lax.broadcasted_iota(jnp.int32, (1, W), 1)
    ci = lax.broadcasted_iota(jnp.int32, (CHUNK, W), 0)
    cj = lax.broadcasted_iota(jnp.int32, (CHUNK, W), 1) % CHUNK
    causal = (ci >= cj).astype(F32)
    bdm = (lax.broadcasted_iota(jnp.int32, (W, LANES), 0) // HEAD
           == lax.broadcasted_iota(jnp.int32, (W, LANES), 1) // D_KEY).astype(F32)
    state = st[...]
    od_parts = []
    for c in range(TB // CHUNK):
        r0 = c * CHUNK
        bc = bcum[r0:r0 + CHUNK, :]
        bl = bc[CHUNK - 1:CHUNK, :]
        qt = (dq[r0:r0 + CHUNK, :] * jnp.exp(bc)).astype(BF16)
        kc = dk[r0:r0 + CHUNK, :]
        kt = kc * jnp.exp(-bc)
        ke = (kc * jnp.exp(bl - bc)).astype(BF16)
        vc = dv[r0:r0 + CHUNK, :]
        kst = jnp.concatenate([kt * (lane // D_KEY == h).astype(F32) for h in range(D_HEADS)],
                              axis=0).astype(BF16)
        att = (_dot_nt(qt, kst) * causal).astype(BF16)
        vbd = jnp.concatenate([vc * (lane256 // HEAD == h).astype(F32) for h in range(D_HEADS)],
                              axis=0).astype(BF16)
        o_c = _dot(att, vbd) + _dot_nt(qt, state.astype(BF16))
        state = state * jnp.exp(bl) + _dot_tn(vc.astype(BF16), ke) * bdm
        od_parts.append(o_c)
    st[...] = state
    od = jnp.concatenate(od_parts, axis=0)
    odn = od * lax.rsqrt(_dot_split(od * od, gmat, 1) + EPS) * p256_ref[R_DNG:R_DNG + 1, :]
    ycat[:, 3 * W:4 * W] = (odn * _silu(dgate)).astype(BF16)

    o_ref[0] = x + _dot(ycat[...], wout_ref[...])


def _t5_bucket(dist):
    max_exact = N_BUCKETS // 2
    d = jnp.maximum(dist, 1).astype(F32)
    large = max_exact + (jnp.log(d / max_exact) / math.log(MAX_DISTANCE / max_exact)
                         * (N_BUCKETS - max_exact)).astype(jnp.int32)
    large = jnp.minimum(large, N_BUCKETS - 1)
    return jnp.where(dist < max_exact, dist, large)


def _layer_spec(layer, shape):
    return pl.BlockSpec((None,) + shape, lambda b, t: (layer,) + (0,) * len(shape))


def _layer_call(layer, x, ng, win, wout, apw, cwr, cwi, dwup, aconv, p256, p128, bucket, sinks, relb):
    bsz, seq, _ = x.shape
    smem = pl.BlockSpec(memory_space=pltpu.SMEM)
    spec = functools.partial(_layer_spec, layer)
    return pl.pallas_call(
        functools.partial(_layer_kernel, layer),
        grid=(bsz, seq // TB),
        in_specs=[
            pl.BlockSpec((1, TB, D_MODEL), lambda b, t: (b, t, 0)),
            spec((1, D_MODEL)), spec((D_MODEL, IN_PAD)), spec((D_MODEL, D_MODEL)),
            spec((W, W)), spec((W, W)), spec((W, W)), spec((LANES, LANES)),
            spec((A_PAD, W)), spec((16, W)), spec((SUBLANES, LANES)),
            pl.BlockSpec((QBLK, 2 * QBLK), lambda b, t: (0, 0)),
            smem, smem,
        ],
        out_specs=pl.BlockSpec((1, TB, D_MODEL), lambda b, t: (b, t, 0)),
        out_shape=jax.ShapeDtypeStruct(x.shape, F32),
        scratch_shapes=[
            pltpu.VMEM((TB + A_PAD, W), F32),
            pltpu.VMEM((TB + C_PAD, W), F32),
            pltpu.VMEM((SUBLANES, W), F32),
            pltpu.VMEM((TB + QBLK, LANES), BF16),
            pltpu.VMEM((TB + QBLK, LANES), BF16),
            pltpu.VMEM((W, LANES), F32),
            pltpu.VMEM((TB, D_MODEL), BF16),
            pltpu.VMEM((4 * QBLK, 2 * QBLK), F32),
            pltpu.VMEM((W, W), BF16),
            pltpu.VMEM((TB, TB), BF16),
            pltpu.VMEM((TB, A_WID), F32),
            pltpu.VMEM((TB, B_WID), F32),
            pltpu.VMEM((TB, C_WID), F32),
            pltpu.VMEM((TB, D_WID), F32),
        ],
        compiler_params=pltpu.CompilerParams(
            dimension_semantics=("arbitrary", "arbitrary"),
            vmem_limit_bytes=48 * 1024 * 1024,
        ),
        name="hybrid_layer",
    )(x, ng, win, wout, apw, cwr, cwi, dwup, aconv, p256, p128, bucket, sinks, relb)


def _block_diag(blocks):
    n, c, _ = blocks.shape
    eye = jnp.eye(n, dtype=blocks.dtype)
    return (eye[:, None, :, None] * blocks[:, :, None, :]).reshape(n * c, n * c)


def _head_perm(w, axis):
    parts = jnp.split(w, 4, axis=axis)
    return jnp.concatenate([parts[0], parts[2], parts[1], parts[3]], axis=axis)


def kernel(x, norm_g, w_in, a_conv_w, a_conv_b, a_ln_g, a_ln_b, a_pw, b_q_g, b_k_g, b_sinks, rel_bias,
           c_conv_w, c_conv_b, c_w_r, c_b_r, c_w_i, c_b_i, c_lambda, d_w_up, d_b_up, d_norm_g, w_out):
    dist = jnp.arange(QBLK)[:, None] + QBLK - jnp.arange(2 * QBLK)[None, :]
    bucket = _t5_bucket(jnp.clip(dist, 0, None)).astype(jnp.int32)
    relb = rel_bias.reshape(-1)
    sinks = b_sinks.reshape(-1)
    win = jnp.concatenate([
        w_in[:, :, 0:768],
        _head_perm(w_in[:, :, 768:1024], 2), w_in[:, :, 1024:1280], _head_perm(w_in[:, :, 1280:1536], 2),
        w_in[:, :, 1536:2048],
        w_in[:, :, 2048:2560], w_in[:, :, 2576:2832], w_in[:, :, 2560:2576],
        jnp.zeros((DEPTH, D_MODEL, IN_PAD - 2832), F32),
    ], axis=2).astype(BF16)
    wout = jnp.concatenate([w_out[:, 0:256], _head_perm(w_out[:, 256:512], 1), w_out[:, 512:1024]],
                           axis=1).astype(BF16)
    dwup = jnp.concatenate([d_w_up, jnp.zeros((DEPTH, LANES - GATE_RANK, LANES), F32)], axis=1).astype(BF16)
    aconv = jnp.concatenate([a_conv_w, jnp.zeros((DEPTH, A_PAD - CONV_A, W), F32)], axis=1)
    row = lambda v: v[:, None, :]
    p256 = jnp.concatenate([
        row(a_conv_b), row(a_ln_g), row(a_ln_b), row(c_conv_b), row(c_b_r), row(c_b_i), row(c_lambda),
        row(jnp.tile(d_norm_g, (1, 4))), c_conv_w, row(jnp.tile(b_q_g, (1, 4))),
        jnp.zeros((DEPTH, 3, W), F32)], axis=1)
    p128 = jnp.concatenate([row(jnp.tile(b_k_g, (1, 2))), row(d_b_up),
                            jnp.zeros((DEPTH, SUBLANES - 2, LANES), F32)], axis=1)
    ng = row(norm_g)
    apw = a_pw.astype(BF16)
    cwr = jax.vmap(_block_diag)(c_w_r).astype(BF16)
    cwi = jax.vmap(_block_diag)(c_w_i).astype(BF16)
    for l in range(DEPTH):
        x = _layer_call(l, x, ng, win, wout, apw, cwr, cwi, dwup, aconv, p256, p128, bucket, sinks, relb)
    return x
```

```python
import functools
import math

import jax
import jax.numpy as jnp
from jax import lax
from jax.experimental import pallas as pl
from jax.experimental.pallas import tpu as pltpu

F32 = jnp.float32
BF16 = jnp.bfloat16

D_MODEL = 1024
DEPTH = 4
W = 256
EPS = 1e-6
CONV_A = 31
A_PAD = 32
HEAD = 64
WINDOW = 128
QBLK = 128
N_BUCKETS = 32
MAX_DISTANCE = 128
CONV_C = 4
C_PAD = 8
LRU_C = 8.0
D_HEADS = 4
D_KEY = 32
GATE_RANK = 16
GATE_TAU = 16.0
CHUNK = 64
SUBLANES = 8
LANES = 128
VMEM_LIMIT_BYTES = 48 * 1024 * 1024

TB = 256

A_OFF, A_WID = 0, 768
B_OFF, B_WID = 768, 768
C_OFF, C_WID = 1536, 512
D_OFF, D_QKV = 2048, 512
D_TAIL_SRC = D_OFF + D_QKV
D_TAIL = 384
D_WID = D_QKV + D_TAIL
IN_MAIN = D_OFF + D_QKV
IN_COLS = 2832


def _dot(a, b):
    return jnp.dot(a, b, preferred_element_type=F32)


def _dot_nt(a, b):
    return lax.dot_general(a, b, (((1,), (1,)), ((), ())), preferred_element_type=F32)


def _dot_tn(a, b):
    return lax.dot_general(a, b, (((0,), (0,)), ((), ())), preferred_element_type=F32)


def _dot_split(x, w_bf, passes):
    acc = None
    r = x
    for p in range(passes):
        part = r.astype(BF16)
        d = _dot(part, w_bf)
        acc = d if acc is None else acc + d
        if p + 1 < passes:
            r = r - part.astype(F32)
    return acc


def _sigmoid(x):
    return 1.0 / (1.0 + jnp.exp(-x))


def _silu(x):
    return x * _sigmoid(x)


def _causal_taps(acc, buf, w_ref, n_taps, pad):
    rows = buf.shape[0]
    for res in range(SUBLANES):
        shifted = None
        for j in range(n_taps):
            off = pad - (n_taps - 1) + j
            if off % SUBLANES != res:
                continue
            if shifted is None:
                shifted = buf if res == 0 else pltpu.roll(buf, rows - res, axis=0)
            base = off - res
            acc = acc + w_ref[j:j + 1, :] * shifted[base:base + TB]
    return acc


def _shift_rows(x, step, fill):
    if step % SUBLANES == 0:
        return jnp.concatenate([jnp.full((step, x.shape[1]), fill, F32), x[0:x.shape[0] - step]], axis=0)
    rolled = pltpu.roll(x, step, axis=0)
    row = lax.broadcasted_iota(jnp.int32, (SUBLANES, x.shape[1]), 0)
    head = jnp.where(row < step, fill, rolled[0:SUBLANES])
    return jnp.concatenate([head, rolled[SUBLANES:]], axis=0)


def _layer_kernel(layer,
                  x_ref, ng_ref, win_ref, wtail_ref, wout_ref, apw_ref, cwr_ref, cwi_ref, dwup_ref, aconv_ref,
                  acb_ref, alg_ref, alb_ref, ccw_ref, ccb_ref, cbr_ref, cbi_ref, lam_ref, dbu_ref,
                  dng_ref, bqg_ref, bkg_ref, bucket_ref, sinks_ref, relb_ref,
                  o_ref,
                  ubuf, cbuf, hcar, kbuf, vbuf, st, ycat, biasm, g256, tri, dwup, pa, pb, pc, pd):
    b_idx = pl.program_id(0)
    t_idx = pl.program_id(1)
    prow = lambda ref: ref[layer:layer + 1, :]

    @pl.when((b_idx == 0) & (t_idx == 0))
    def _build_tables():
        ri = lax.broadcasted_iota(jnp.int32, (W, W), 0) // HEAD
        ci = lax.broadcasted_iota(jnp.int32, (W, W), 1) // HEAD
        g256[...] = jnp.where(ri == ci, 1.0 / HEAD, 0.0).astype(BF16)
        rt = lax.broadcasted_iota(jnp.int32, (TB, TB), 0)
        ct = lax.broadcasted_iota(jnp.int32, (TB, TB), 1)
        tri[...] = jnp.where((rt // CHUNK == ct // CHUNK) & (rt >= ct), 1.0, 0.0).astype(BF16)
        dwup[...] = jnp.zeros((LANES, LANES), BF16)
        dwup[0:GATE_RANK, :] = dwup_ref[...].astype(BF16)
        bucket = bucket_ref[...]
        qi = lax.broadcasted_iota(jnp.int32, (QBLK, 2 * QBLK), 0)
        kj = lax.broadcasted_iota(jnp.int32, (QBLK, 2 * QBLK), 1)
        dist = qi + QBLK - kj
        valid = (dist >= 0) & (dist < WINDOW)
        for h in range(4):
            acc = jnp.zeros((QBLK, 2 * QBLK), F32)
            for bk in range(N_BUCKETS):
                acc = jnp.where(bucket == bk, relb_ref[bk * 4 + h], acc)
            biasm[h * QBLK:(h + 1) * QBLK, :] = jnp.where(valid, acc, -jnp.inf)

    @pl.when(t_idx == 0)
    def _reset_state():
        ubuf[0:A_PAD, :] = jnp.zeros((A_PAD, W), F32)
        cbuf[0:C_PAD, :] = jnp.zeros((C_PAD, W), F32)
        hcar[...] = jnp.zeros((SUBLANES, W), F32)
        kbuf[0:QBLK, :] = jnp.zeros((QBLK, LANES), BF16)
        vbuf[0:QBLK, :] = jnp.zeros((QBLK, LANES), BF16)
        st[...] = jnp.zeros((W, LANES), F32)

    x = x_ref[0]
    ms = jnp.mean(x * x, axis=-1, keepdims=True)
    hb = (x * lax.rsqrt(ms + EPS) * prow(ng_ref)).astype(BF16)

    gmat = g256[...]

    pa[...] = _dot(hb, win_ref[:, A_OFF:A_OFF + A_WID])
    pb[...] = _dot(hb, win_ref[:, B_OFF:B_OFF + B_WID])
    pc[...] = _dot(hb, win_ref[:, C_OFF:C_OFF + C_WID])
    pd[:, 0:D_QKV] = _dot(hb, win_ref[:, D_OFF:D_OFF + D_QKV])
    pd[:, D_QKV:D_WID] = _dot(hb, wtail_ref[...])

    u = pa[:, 0:W] * _sigmoid(pa[:, W:2 * W])
    ubuf[A_PAD:A_PAD + TB, :] = u
    conv = jnp.broadcast_to(prow(acb_ref), (TB, W))
    conv = _causal_taps(conv, ubuf[...], aconv_ref, CONV_A, A_PAD)
    ubuf[0:A_PAD, :] = ubuf[TB:TB + A_PAD, :]
    mu = _dot_split(conv, gmat, 2)
    dc = conv - mu
    var = _dot_split(dc * dc, gmat, 1)
    un = dc * lax.rsqrt(var + EPS) * prow(alg_ref) + prow(alb_ref)
    ya = _dot(_silu(un).astype(BF16), apw_ref[...]) * _silu(pa[:, 2 * W:3 * W])
    ycat[:, 0:W] = ya.astype(BF16)

    q = pb[:, 0:W]
    k = pb[:, W:W + LANES]
    v = pb[:, W + LANES:W + 2 * LANES]
    qn = q * lax.rsqrt(_dot_split(q * q, gmat, 1) + EPS) * prow(bqg_ref)
    kn = k * lax.rsqrt(_dot_split(k * k, gmat[0:LANES, 0:LANES], 1) + EPS) * prow(bkg_ref)
    kbuf[QBLK:QBLK + TB, :] = kn.astype(BF16)
    vbuf[QBLK:QBLK + TB, :] = v.astype(BF16)
    lane = lax.broadcasted_iota(jnp.int32, (1, LANES), 1)
    lanem = lane < HEAD
    scale = HEAD ** -0.5
    lo = jnp.where(lanem, scale, 0.0)
    hi = jnp.where(lanem, 0.0, scale)
    first_tile_mask = jnp.where(t_idx == 0, -jnp.inf, 0.0)

    def softmax_rows(sh, h, bi):
        if bi == 0:
            sh = jnp.concatenate([sh[:, 0:QBLK] + first_tile_mask, sh[:, QBLK:]], axis=1)
        sink = sinks_ref[layer * 4 + h]
        m = jnp.maximum(jnp.max(sh, axis=-1, keepdims=True), sink)
        p = jnp.exp(sh - m)
        den = jnp.sum(p, axis=-1, keepdims=True) + jnp.exp(sink - m)
        return p.astype(BF16), den

    for bi in range(TB // QBLK):
        r0 = bi * QBLK
        qa = qn[r0:r0 + QBLK, 0:LANES]
        qb = qn[r0:r0 + QBLK, LANES:2 * LANES]
        qst = jnp.concatenate([qa * lo, pltpu.roll(qa, HEAD, axis=1) * lo,
                               pltpu.roll(qb, HEAD, axis=1) * hi, qb * hi], axis=0).astype(BF16)
        s = _dot_nt(qst, kbuf[r0:r0 + 2 * QBLK, :]) + biasm[...]
        pd_ = [softmax_rows(s[h * QBLK:(h + 1) * QBLK, :], h, bi) for h in range(4)]
        ov = _dot(jnp.concatenate([t[0] for t in pd_], axis=0), vbuf[r0:r0 + 2 * QBLK, :])
        oh = [ov[h * QBLK:(h + 1) * QBLK, :] / pd_[h][1] for h in range(4)]
        y01 = jnp.where(lanem, oh[0], pltpu.roll(oh[1], HEAD, axis=1))
        y23 = jnp.where(lanem, pltpu.roll(oh[2], HEAD, axis=1), oh[3])
        gate = pb[r0:r0 + QBLK, W + 2 * LANES:2 * W + 2 * LANES]
        ycat[r0:r0 + QBLK, W:2 * W] = (jnp.concatenate([y01, y23], axis=1) * _silu(gate)).astype(BF16)
    kbuf[0:QBLK, :] = kbuf[TB:TB + QBLK, :]
    vbuf[0:QBLK, :] = vbuf[TB:TB + QBLK, :]

    cbuf[C_PAD:C_PAD + TB, :] = pc[:, 0:W]
    xc = jnp.broadcast_to(prow(ccb_ref), (TB, W))
    xc = _causal_taps(xc, cbuf[...], ccw_ref, CONV_C, C_PAD)
    cbuf[0:C_PAD, :] = cbuf[TB:TB + C_PAD, :]
    xcb = xc.astype(BF16)
    rg = _sigmoid(_dot(xcb, cwr_ref[...]) + prow(cbr_ref))
    ig = _sigmoid(_dot(xcb, cwi_ref[...]) + prow(cbi_ref))
    nlam = -prow(lam_ref)
    softplus = jnp.maximum(nlam, 0.0) + jnp.log1p(jnp.exp(-jnp.abs(nlam)))
    log_a = (-LRU_C * rg) * softplus
    a = jnp.exp(log_a)
    uu = jnp.sqrt(1.0 - jnp.exp(2.0 * log_a)) * (ig * xc)
    step = 1
    while step < TB:
        uu = a * _shift_rows(uu, step, 0.0) + uu
        a = a * _shift_rows(a, step, 1.0)
        step *= 2
    hs = uu + a * hcar[0:1, :]
    hcar[0:1, :] = hs[TB - 1:TB, :]
    ycat[:, 2 * W:3 * W] = (hs * _silu(pc[:, W:2 * W])).astype(BF16)

    dq = pd[:, 0:LANES] * (D_KEY ** -0.5)
    dk = pd[:, LANES:2 * LANES]
    dv = pd[:, 2 * LANES:2 * LANES + W]
    dgate = pd[:, D_QKV:D_QKV + W]
    dlr = pd[:, D_QKV + W:D_QKV + W + LANES]
    z = _dot(dlr.astype(BF16), dwup[...]) + prow(dbu_ref)
    lg = (jnp.minimum(z, 0.0) - jnp.log1p(jnp.exp(-jnp.abs(z)))) * (1.0 / GATE_TAU)
    bcum = None
    r = lg
    for p_ in range(2):
        part = r.astype(BF16)
        d_ = _dot(tri[...], part)
        bcum = d_ if bcum is None else bcum + d_
        r = r - part.astype(F32)
    lane256 = lax.broadcasted_iota(jnp.int32, (1, W), 1)
    ci = lax.broadcasted_iota(jnp.int32, (CHUNK, W), 0)
    cj = lax.broadcasted_iota(jnp.int32, (CHUNK, W), 1) % CHUNK
    causal = (ci >= cj).astype(F32)
    bdm = (lax.broadcasted_iota(jnp.int32, (W, LANES), 0) // HEAD
           == lax.broadcasted_iota(jnp.int32, (W, LANES), 1) // D_KEY).astype(F32)
    state = st[...]
    od_parts = []
    for c in range(TB // CHUNK):
        r0 = c * CHUNK
        bc = bcum[r0:r0 + CHUNK, :]
        bl = bc[CHUNK - 1:CHUNK, :]
        qt = (dq[r0:r0 + CHUNK, :] * jnp.exp(bc)).astype(BF16)
        kc = dk[r0:r0 + CHUNK, :]
        kt = kc * jnp.exp(-bc)
        ke = (kc * jnp.exp(bl - bc)).astype(BF16)
        vc = dv[r0:r0 + CHUNK, :]
        kst = jnp.concatenate([kt * (lane // D_KEY == h).astype(F32) for h in range(D_HEADS)],
                              axis=0).astype(BF16)
        att = (_dot_nt(qt, kst) * causal).astype(BF16)
        vbd = jnp.concatenate([vc * (lane256 // HEAD == h).astype(F32) for h in range(D_HEADS)],
                              axis=0).astype(BF16)
        o_c = _dot(att, vbd) + _dot_nt(qt, state.astype(BF16))
        state = state * jnp.exp(bl) + _dot_tn(vc.astype(BF16), ke) * bdm
        od_parts.append(o_c)
    st[...] = state
    od = jnp.concatenate(od_parts, axis=0)
    odn = od * lax.rsqrt(_dot_split(od * od, gmat, 1) + EPS) * prow(dng_ref)
    ycat[:, 3 * W:4 * W] = (odn * _silu(dgate)).astype(BF16)

    o_ref[0] = x + _dot(ycat[...], wout_ref[...])


def _t5_bucket(dist):
    max_exact = N_BUCKETS // 2
    d = jnp.maximum(dist, 1).astype(F32)
    large = max_exact + (jnp.log(d / max_exact) / math.log(MAX_DISTANCE / max_exact)
                         * (N_BUCKETS - max_exact)).astype(jnp.int32)
    large = jnp.minimum(large, N_BUCKETS - 1)
    return jnp.where(dist < max_exact, dist, large)


def _layer_spec(layer, shape):
    return pl.BlockSpec((None,) + shape, lambda b, t: (layer,) + (0,) * len(shape))


def _whole(shape):
    return pl.BlockSpec(shape, lambda b, t: (0,) * len(shape))


def _layer_call(layer, x, ng, win, wtail, wout, apw, cwr, cwi, dwup, aconv, vec256, ccw, dbu, dng, bqg, bkg,
                bucket, sinks, relb):
    bsz, seq, _ = x.shape
    smem = pl.BlockSpec(memory_space=pltpu.SMEM)
    spec = functools.partial(_layer_spec, layer)
    acb, alg, alb, ccb, cbr, cbi, lam = vec256
    return pl.pallas_call(
        functools.partial(_layer_kernel, layer),
        grid=(bsz, seq // TB),
        in_specs=[
            pl.BlockSpec((1, TB, D_MODEL), lambda b, t: (b, t, 0)),
            _whole((DEPTH, D_MODEL)), spec((D_MODEL, IN_MAIN)), spec((D_MODEL, D_TAIL)),
            spec((D_MODEL, D_MODEL)), spec((W, W)), spec((W, W)), spec((W, W)),
            spec((GATE_RANK, LANES)), spec((CONV_A, W)),
            _whole((DEPTH, W)), _whole((DEPTH, W)), _whole((DEPTH, W)),
            spec((CONV_C, W)),
            _whole((DEPTH, W)), _whole((DEPTH, W)), _whole((DEPTH, W)), _whole((DEPTH, W)),
            _whole((DEPTH, LANES)),
            _whole((DEPTH, W)), _whole((DEPTH, W)), _whole((DEPTH, LANES)),
            _whole((QBLK, 2 * QBLK)),
            smem, smem,
        ],
        out_specs=pl.BlockSpec((1, TB, D_MODEL), lambda b, t: (b, t, 0)),
        out_shape=jax.ShapeDtypeStruct(x.shape, F32),
        scratch_shapes=[
            pltpu.VMEM((TB + A_PAD, W), F32),
            pltpu.VMEM((TB + C_PAD, W), F32),
            pltpu.VMEM((SUBLANES, W), F32),
            pltpu.VMEM((TB + QBLK, LANES), BF16),
            pltpu.VMEM((TB + QBLK, LANES), BF16),
            pltpu.VMEM((W, LANES), F32),
            pltpu.VMEM((TB, D_MODEL), BF16),
            pltpu.VMEM((4 * QBLK, 2 * QBLK), F32),
            pltpu.VMEM((W, W), BF16),
            pltpu.VMEM((TB, TB), BF16),
            pltpu.VMEM((LANES, LANES), BF16),
            pltpu.VMEM((TB, A_WID), F32),
            pltpu.VMEM((TB, B_WID), F32),
            pltpu.VMEM((TB, C_WID), F32),
            pltpu.VMEM((TB, D_WID), F32),
        ],
        compiler_params=pltpu.CompilerParams(
            dimension_semantics=("arbitrary", "arbitrary"),
            vmem_limit_bytes=VMEM_LIMIT_BYTES,
        ),
        name="hybrid_layer",
    )(x, ng, win, wtail, wout, apw, cwr, cwi, dwup, aconv, acb, alg, alb, ccw, ccb, cbr, cbi, lam, dbu,
      dng, bqg, bkg, bucket, sinks, relb)


def _block_diag(blocks):
    depth, n, c, _ = blocks.shape
    eye = jnp.eye(n, dtype=blocks.dtype)
    return (eye[None, :, None, :, None] * blocks[:, :, :, None, :]).reshape(depth, n * c, n * c)


def kernel(x, norm_g, w_in, a_conv_w, a_conv_b, a_ln_g, a_ln_b, a_pw, b_q_g, b_k_g, b_sinks, rel_bias,
           c_conv_w, c_conv_b, c_w_r, c_b_r, c_w_i, c_b_i, c_lambda, d_w_up, d_b_up, d_norm_g, w_out):
    dist = jnp.arange(QBLK)[:, None] + QBLK - jnp.arange(2 * QBLK)[None, :]
    bucket = _t5_bucket(jnp.clip(dist, 0, None)).astype(jnp.int32)
    win = w_in[:, :, 0:IN_MAIN].astype(BF16)
    wtail = jnp.concatenate([
        w_in[:, :, D_TAIL_SRC + GATE_RANK:IN_COLS], w_in[:, :, D_TAIL_SRC:D_TAIL_SRC + GATE_RANK],
        jnp.zeros((DEPTH, D_MODEL, D_TAIL - (IN_COLS - D_TAIL_SRC)), F32)], axis=2).astype(BF16)
    vec256 = (a_conv_b, a_ln_g, a_ln_b, c_conv_b, c_b_r, c_b_i, c_lambda)
    args = (norm_g, win, wtail, w_out.astype(BF16), a_pw.astype(BF16),
            _block_diag(c_w_r).astype(BF16), _block_diag(c_w_i).astype(BF16), d_w_up, a_conv_w,
            vec256, c_conv_w, d_b_up, jnp.tile(d_norm_g, (1, 4)), jnp.tile(b_q_g, (1, 4)),
            jnp.tile(b_k_g, (1, 2)), bucket, b_sinks.reshape(-1), rel_bias.reshape(-1))
    for l in range(DEPTH):
        x = _layer_call(l, x, *args)
    return x
```

```python
import functools
import math

import jax
import jax.numpy as jnp
from jax import lax
from jax.experimental import pallas as pl
from jax.experimental.pallas import tpu as pltpu

F32 = jnp.float32
BF16 = jnp.bfloat16

D_MODEL = 1024
DEPTH = 4
W = 256
EPS = 1e-6
CONV_A = 31
A_PAD = 32
HEAD = 64
WINDOW = 128
QBLK = 128
N_BUCKETS = 32
MAX_DISTANCE = 128
CONV_C = 4
C_PAD = 8
LRU_C = 8.0
D_HEADS = 4
D_KEY = 32
GATE_RANK = 16
GATE_TAU = 16.0
CHUNK = 64
SUBLANES = 8
LANES = 128
VMEM_LIMIT_BYTES = 48 * 1024 * 1024

TB = 256

A_OFF = 0
B_OFF = 768
B_QKV = 512
C_OFF = 1536
D_OFF, D_QKV = 2048, 512
D_TAIL_SRC = D_OFF + D_QKV
D_TAIL = 384
IN_COLS = 2832


def _dot(a, b):
    return jnp.dot(a, b, preferred_element_type=F32)


def _dot_nt(a, b):
    return lax.dot_general(a, b, (((1,), (1,)), ((), ())), preferred_element_type=F32)


def _dot_tn(a, b):
    return lax.dot_general(a, b, (((0,), (0,)), ((), ())), preferred_element_type=F32)


def _dot_split(x, w_bf, passes):
    acc = None
    r = x
    for p in range(passes):
        part = r.astype(BF16)
        d = _dot(part, w_bf)
        acc = d if acc is None else acc + d
        if p + 1 < passes:
            r = r - part.astype(F32)
    return acc


def _sigmoid(x):
    return 1.0 / (1.0 + jnp.exp(-x))


def _silu(x):
    return x * _sigmoid(x)


def _causal_taps(acc, buf, w_ref, n_taps, pad):
    rows = buf.shape[0]
    for res in range(SUBLANES):
        shifted = None
        for j in range(n_taps):
            off = pad - (n_taps - 1) + j
            if off % SUBLANES != res:
                continue
            if shifted is None:
                shifted = buf if res == 0 else pltpu.roll(buf, rows - res, axis=0)
            base = off - res
            acc = acc + w_ref[j:j + 1, :] * shifted[base:base + TB]
    return acc


def _shift_rows(x, step, fill):
    if step % SUBLANES == 0:
        return jnp.concatenate([jnp.full((step, x.shape[1]), fill, F32), x[0:x.shape[0] - step]], axis=0)
    rolled = pltpu.roll(x, step, axis=0)
    row = lax.broadcasted_iota(jnp.int32, (SUBLANES, x.shape[1]), 0)
    head = jnp.where(row < step, fill, rolled[0:SUBLANES])
    return jnp.concatenate([head, rolled[SUBLANES:]], axis=0)


def _layer_kernel(layer,
                  x_ref, ng_ref, win_ref, wtail_ref, wout_ref, apw_ref, cwr_ref, cwi_ref, dwup_ref, aconv_ref,
                  acb_ref, alg_ref, alb_ref, ccw_ref, ccb_ref, cbr_ref, cbi_ref, lam_ref, dbu_ref,
                  dng_ref, bqg_ref, bkg_ref, bucket_ref, sinks_ref, relb_ref,
                  o_ref,
                  ubuf, cbuf, hcar, kbuf, vbuf, st, ycat, biasm, g256, tri, dwup, gates, pb, pd, plr):
    b_idx = pl.program_id(0)
    t_idx = pl.program_id(1)
    prow = lambda ref: ref[layer:layer + 1, :]

    @pl.when((b_idx == 0) & (t_idx == 0))
    def _build_tables():
        ri = lax.broadcasted_iota(jnp.int32, (W, W), 0) // HEAD
        ci = lax.broadcasted_iota(jnp.int32, (W, W), 1) // HEAD
        g256[...] = jnp.where(ri == ci, 1.0 / HEAD, 0.0).astype(BF16)
        rt = lax.broadcasted_iota(jnp.int32, (TB, TB), 0)
        ct = lax.broadcasted_iota(jnp.int32, (TB, TB), 1)
        tri[...] = jnp.where((rt // CHUNK == ct // CHUNK) & (rt >= ct), 1.0, 0.0).astype(BF16)
        dwup[...] = jnp.zeros((LANES, LANES), BF16)
        dwup[0:GATE_RANK, :] = dwup_ref[...].astype(BF16)
        bucket = bucket_ref[...]
        qi = lax.broadcasted_iota(jnp.int32, (QBLK, 2 * QBLK), 0)
        kj = lax.broadcasted_iota(jnp.int32, (QBLK, 2 * QBLK), 1)
        dist = qi + QBLK - kj
        valid = (dist >= 0) & (dist < WINDOW)
        for h in range(4):
            acc = jnp.zeros((QBLK, 2 * QBLK), F32)
            for bk in range(N_BUCKETS):
                acc = jnp.where(bucket == bk, relb_ref[bk * 4 + h], acc)
            biasm[h * QBLK:(h + 1) * QBLK, :] = jnp.where(valid, acc, -jnp.inf)

    @pl.when(t_idx == 0)
    def _reset_state():
        ubuf[0:A_PAD, :] = jnp.zeros((A_PAD, W), F32)
        cbuf[0:C_PAD, :] = jnp.zeros((C_PAD, W), F32)
        hcar[...] = jnp.zeros((SUBLANES, W), F32)
        kbuf[0:QBLK, :] = jnp.zeros((QBLK, LANES), BF16)
        vbuf[0:QBLK, :] = jnp.zeros((QBLK, LANES), BF16)
        st[...] = jnp.zeros((W, LANES), F32)

    x = x_ref[0]
    ms = jnp.mean(x * x, axis=-1, keepdims=True)
    hb = (x * lax.rsqrt(ms + EPS) * prow(ng_ref)).astype(BF16)

    gmat = g256[...]

    proj = lambda off, width: _dot(hb, win_ref[:, off:off + width])
    pav = proj(A_OFF, 2 * W)
    ubuf[A_PAD:A_PAD + TB, :] = pav[:, 0:W] * _sigmoid(pav[:, W:2 * W])
    gates[:, 0:W] = _silu(proj(A_OFF + 2 * W, W))
    pb[...] = proj(B_OFF, B_QKV)
    gates[:, W:2 * W] = _silu(proj(B_OFF + B_QKV, W))
    cbuf[C_PAD:C_PAD + TB, :] = proj(C_OFF, W)
    gates[:, 2 * W:3 * W] = _silu(proj(C_OFF + W, W))
    pd[...] = proj(D_OFF, D_QKV)
    ptail = _dot(hb, wtail_ref[...])
    gates[:, 3 * W:4 * W] = _silu(ptail[:, 0:W])
    plr[...] = ptail[:, W:W + LANES]

    conv = jnp.broadcast_to(prow(acb_ref), (TB, W))
    conv = _causal_taps(conv, ubuf[...], aconv_ref, CONV_A, A_PAD)
    ubuf[0:A_PAD, :] = ubuf[TB:TB + A_PAD, :]
    mu = _dot_split(conv, gmat, 2)
    dc = conv - mu
    var = _dot_split(dc * dc, gmat, 1)
    un = dc * lax.rsqrt(var + EPS) * prow(alg_ref) + prow(alb_ref)
    ya = _dot(_silu(un).astype(BF16), apw_ref[...]) * gates[:, 0:W]
    ycat[:, 0:W] = ya.astype(BF16)

    q = pb[:, 0:W]
    k = pb[:, W:W + LANES]
    v = pb[:, W + LANES:W + 2 * LANES]
    qn = q * lax.rsqrt(_dot_split(q * q, gmat, 1) + EPS) * prow(bqg_ref)
    kn = k * lax.rsqrt(_dot_split(k * k, gmat[0:LANES, 0:LANES], 1) + EPS) * prow(bkg_ref)
    kbuf[QBLK:QBLK + TB, :] = kn.astype(BF16)
    vbuf[QBLK:QBLK + TB, :] = v.astype(BF16)
    lane = lax.broadcasted_iota(jnp.int32, (1, LANES), 1)
    lanem = lane < HEAD
    scale = HEAD ** -0.5
    lo = jnp.where(lanem, scale, 0.0)
    hi = jnp.where(lanem, 0.0, scale)
    first_tile_mask = jnp.where(t_idx == 0, -jnp.inf, 0.0)

    def softmax_rows(sh, h, bi):
        if bi == 0:
            sh = jnp.concatenate([sh[:, 0:QBLK] + first_tile_mask, sh[:, QBLK:]], axis=1)
        sink = sinks_ref[layer * 4 + h]
        m = jnp.maximum(jnp.max(sh, axis=-1, keepdims=True), sink)
        p = jnp.exp(sh - m)
        den = jnp.sum(p, axis=-1, keepdims=True) + jnp.exp(sink - m)
        return p.astype(BF16), den

    for bi in range(TB // QBLK):
        r0 = bi * QBLK
        qa = qn[r0:r0 + QBLK, 0:LANES]
        qb = qn[r0:r0 + QBLK, LANES:2 * LANES]
        qst = jnp.concatenate([qa * lo, pltpu.roll(qa, HEAD, axis=1) * lo,
                               pltpu.roll(qb, HEAD, axis=1) * hi, qb * hi], axis=0).astype(BF16)
        s = _dot_nt(qst, kbuf[r0:r0 + 2 * QBLK, :]) + biasm[...]
        pden = [softmax_rows(s[h * QBLK:(h + 1) * QBLK, :], h, bi) for h in range(4)]
        ov = _dot(jnp.concatenate([t[0] for t in pden], axis=0), vbuf[r0:r0 + 2 * QBLK, :])
        oh = [ov[h * QBLK:(h + 1) * QBLK, :] / pden[h][1] for h in range(4)]
        y01 = jnp.where(lanem, oh[0], pltpu.roll(oh[1], HEAD, axis=1))
        y23 = jnp.where(lanem, pltpu.roll(oh[2], HEAD, axis=1), oh[3])
        ycat[r0:r0 + QBLK, W:2 * W] = (jnp.concatenate([y01, y23], axis=1)
                                       * gates[r0:r0 + QBLK, W:2 * W]).astype(BF16)
    kbuf[0:QBLK, :] = kbuf[TB:TB + QBLK, :]
    vbuf[0:QBLK, :] = vbuf[TB:TB + QBLK, :]

    xc = jnp.broadcast_to(prow(ccb_ref), (TB, W))
    xc = _causal_taps(xc, cbuf[...], ccw_ref, CONV_C, C_PAD)
    cbuf[0:C_PAD, :] = cbuf[TB:TB + C_PAD, :]
    xcb = xc.astype(BF16)
    rg = _sigmoid(_dot(xcb, cwr_ref[...]) + prow(cbr_ref))
    ig = _sigmoid(_dot(xcb, cwi_ref[...]) + prow(cbi_ref))
    nlam = -prow(lam_ref)
    softplus = jnp.maximum(nlam, 0.0) + jnp.log1p(jnp.exp(-jnp.abs(nlam)))
    log_a = (-LRU_C * rg) * softplus
    a = jnp.exp(log_a)
    uu = jnp.sqrt(1.0 - jnp.exp(2.0 * log_a)) * (ig * xc)
    step = 1
    while step < TB:
        uu = a * _shift_rows(uu, step, 0.0) + uu
        a = a * _shift_rows(a, step, 1.0)
        step *= 2
    hs = uu + a * hcar[0:1, :]
    hcar[0:1, :] = hs[TB - 1:TB, :]
    ycat[:, 2 * W:3 * W] = (hs * gates[:, 2 * W:3 * W]).astype(BF16)

    dq = pd[:, 0:LANES] * (D_KEY ** -0.5)
    dk = pd[:, LANES:2 * LANES]
    dv = pd[:, 2 * LANES:2 * LANES + W]
    z = _dot(plr[...].astype(BF16), dwup[...]) + prow(dbu_ref)
    lg = (jnp.minimum(z, 0.0) - jnp.log1p(jnp.exp(-jnp.abs(z)))) * (1.0 / GATE_TAU)
    bcum = None
    r = lg
    for p_ in range(2):
        part = r.astype(BF16)
        d_ = _dot(tri[...], part)
        bcum = d_ if bcum is None else bcum + d_
        r = r - part.astype(F32)
    lane256 = lax.broadcasted_iota(jnp.int32, (1, W), 1)
    ci = lax.broadcasted_iota(jnp.int32, (CHUNK, W), 0)
    cj = lax.broadcasted_iota(jnp.int32, (CHUNK, W), 1) % CHUNK
    causal = (ci >= cj).astype(F32)
    bdm = (lax.broadcasted_iota(jnp.int32, (W, LANES), 0) // HEAD
           == lax.broadcasted_iota(jnp.int32, (W, LANES), 1) // D_KEY).astype(F32)
    state = st[...]
    od_parts = []
    for c in range(TB // CHUNK):
        r0 = c * CHUNK
        bc = bcum[r0:r0 + CHUNK, :]
        bl = bc[CHUNK - 1:CHUNK, :]
        qt = (dq[r0:r0 + CHUNK, :] * jnp.exp(bc)).astype(BF16)
        kc = dk[r0:r0 + CHUNK, :]
        kt = kc * jnp.exp(-bc)
        ke = (kc * jnp.exp(bl - bc)).astype(BF16)
        vc = dv[r0:r0 + CHUNK, :]
        kst = jnp.concatenate([kt * (lane // D_KEY == h).astype(F32) for h in range(D_HEADS)],
                              axis=0).astype(BF16)
        att = (_dot_nt(qt, kst) * causal).astype(BF16)
        vbd = jnp.concatenate([vc * (lane256 // HEAD == h).astype(F32) for h in range(D_HEADS)],
                              axis=0).astype(BF16)
        o_c = _dot(att, vbd) + _dot_nt(qt, state.astype(BF16))
        state = state * jnp.exp(bl) + _dot_tn(vc.astype(BF16), ke) * bdm
        od_parts.append(o_c)
    st[...] = state
    od = jnp.concatenate(od_parts, axis=0)
    odn = od * lax.rsqrt(_dot_split(od * od, gmat, 1) + EPS) * prow(dng_ref)
    ycat[:, 3 * W:4 * W] = (odn * gates[:, 3 * W:4 * W]).astype(BF16)

    o_ref[0] = x + _dot(ycat[...], wout_ref[...])


def _t5_bucket(dist):
    max_exact = N_BUCKETS // 2
    d = jnp.maximum(dist, 1).astype(F32)
    large = max_exact + (jnp.log(d / max_exact) / math.log(MAX_DISTANCE / max_exact)
                         * (N_BUCKETS - max_exact)).astype(jnp.int32)
    large = jnp.minimum(large, N_BUCKETS - 1)
    return jnp.where(dist < max_exact, dist, large)


def _layer_spec(layer, shape):
    return pl.BlockSpec((None,) + shape, lambda b, t: (layer,) + (0,) * len(shape))


def _whole(shape):
    return pl.BlockSpec(shape, lambda b, t: (0,) * len(shape))


def _layer_call(layer, x, ng, win, wtail, wout, apw, cwr, cwi, dwup, aconv, vec256, ccw, dbu, dng, bqg, bkg,
                bucket, sinks, relb):
    bsz, seq, _ = x.shape
    smem = pl.BlockSpec(memory_space=pltpu.SMEM)
    spec = functools.partial(_layer_spec, layer)
    acb, alg, alb, ccb, cbr, cbi, lam = vec256
    return pl.pallas_call(
        functools.partial(_layer_kernel, layer),
        grid=(bsz, seq // TB),
        in_specs=[
            pl.BlockSpec((1, TB, D_MODEL), lambda b, t: (b, t, 0)),
            _whole((DEPTH, D_MODEL)), spec((D_MODEL, IN_COLS)), spec((D_MODEL, D_TAIL)),
            spec((D_MODEL, D_MODEL)), spec((W, W)), spec((W, W)), spec((W, W)),
            spec((GATE_RANK, LANES)), spec((CONV_A, W)),
            _whole((DEPTH, W)), _whole((DEPTH, W)), _whole((DEPTH, W)),
            spec((CONV_C, W)),
            _whole((DEPTH, W)), _whole((DEPTH, W)), _whole((DEPTH, W)), _whole((DEPTH, W)),
            _whole((DEPTH, LANES)),
            _whole((DEPTH, W)), _whole((DEPTH, W)), _whole((DEPTH, LANES)),
            _whole((QBLK, 2 * QBLK)),
            smem, smem,
        ],
        out_specs=pl.BlockSpec((1, TB, D_MODEL), lambda b, t: (b, t, 0)),
        out_shape=jax.ShapeDtypeStruct(x.shape, F32),
        scratch_shapes=[
            pltpu.VMEM((TB + A_PAD, W), F32),
            pltpu.VMEM((TB + C_PAD, W), F32),
            pltpu.VMEM((SUBLANES, W), F32),
            pltpu.VMEM((TB + QBLK, LANES), BF16),
            pltpu.VMEM((TB + QBLK, LANES), BF16),
            pltpu.VMEM((W, LANES), F32),
            pltpu.VMEM((TB, D_MODEL), BF16),
            pltpu.VMEM((4 * QBLK, 2 * QBLK), F32),
            pltpu.VMEM((W, W), BF16),
            pltpu.VMEM((TB, TB), BF16),
            pltpu.VMEM((LANES, LANES), BF16),
            pltpu.VMEM((TB, D_MODEL), F32),
            pltpu.VMEM((TB, B_QKV), F32),
            pltpu.VMEM((TB, D_QKV), F32),
            pltpu.VMEM((TB, LANES), F32),
        ],
        compiler_params=pltpu.CompilerParams(
            dimension_semantics=("arbitrary", "arbitrary"),
            vmem_limit_bytes=VMEM_LIMIT_BYTES,
        ),
        name="hybrid_layer",
    )(x, ng, win, wtail, wout, apw, cwr, cwi, dwup, aconv, acb, alg, alb, ccw, ccb, cbr, cbi, lam, dbu,
      dng, bqg, bkg, bucket, sinks, relb)


def _block_diag(blocks):
    depth, n, c, _ = blocks.shape
    eye = jnp.eye(n, dtype=blocks.dtype)
    return (eye[None, :, None, :, None] * blocks[:, :, :, None, :]).reshape(depth, n * c, n * c)


def kernel(x, norm_g, w_in, a_conv_w, a_conv_b, a_ln_g, a_ln_b, a_pw, b_q_g, b_k_g, b_sinks, rel_bias,
           c_conv_w, c_conv_b, c_w_r, c_b_r, c_w_i, c_b_i, c_lambda, d_w_up, d_b_up, d_norm_g, w_out):
    dist = jnp.arange(QBLK)[:, None] + QBLK - jnp.arange(2 * QBLK)[None, :]
    bucket = _t5_bucket(jnp.clip(dist, 0, None)).astype(jnp.int32)
    wtail = jnp.concatenate([
        w_in[:, :, D_TAIL_SRC + GATE_RANK:IN_COLS], w_in[:, :, D_TAIL_SRC:D_TAIL_SRC + GATE_RANK],
        jnp.zeros((DEPTH, D_MODEL, D_TAIL - (IN_COLS - D_TAIL_SRC)), F32)], axis=2).astype(BF16)
    vec256 = (a_conv_b, a_ln_g, a_ln_b, c_conv_b, c_b_r, c_b_i, c_lambda)
    args = (norm_g, w_in.astype(BF16), wtail, w_out.astype(BF16), a_pw.astype(BF16),
            _block_diag(c_w_r).astype(BF16), _block_diag(c_w_i).astype(BF16), d_w_up, a_conv_w,
            vec256, c_conv_w, d_b_up, jnp.tile(d_norm_g, (1, 4)), jnp.tile(b_q_g, (1, 4)),
            jnp.tile(b_k_g, (1, 2)), bucket, b_sinks.reshape(-1), rel_bias.reshape(-1))
    for l in range(DEPTH):
        x = _layer_call(l, x, *args)
    return x
```

```python
import functools
import math

import jax
import jax.numpy as jnp
from jax import lax
from jax.experimental import pallas as pl
from jax.experimental.pallas import tpu as pltpu

F32 = jnp.float32
BF16 = jnp.bfloat16

D_MODEL = 1024
DEPTH = 4
W = 256
EPS = 1e-6
CONV_A = 31
A_PAD = 32
HEAD = 64
WINDOW = 128
QBLK = 128
N_BUCKETS = 32
MAX_DISTANCE = 128
CONV_C = 4
C_PAD = 8
LRU_C = 8.0
D_HEADS = 4
D_KEY = 32
GATE_RANK = 16
GATE_TAU = 16.0
CHUNK = 64
SUBLANES = 8
LANES = 128
VMEM_LIMIT_BYTES = 48 * 1024 * 1024
LOG2E = math.log2(math.e)

TB = 256

A_OFF = 0
B_OFF = 768
B_QKV = 512
C_OFF = 1536
D_OFF, D_QKV = 2048, 512
D_TAIL_SRC = D_OFF + D_QKV
D_TAIL = 384
IN_COLS = 2832


def _dot(a, b):
    return jnp.dot(a, b, preferred_element_type=F32)


def _dot_nt(a, b):
    return lax.dot_general(a, b, (((1,), (1,)), ((), ())), preferred_element_type=F32)


def _dot_tn(a, b):
    return lax.dot_general(a, b, (((0,), (0,)), ((), ())), preferred_element_type=F32)


def _dot_split(x, w_bf, passes):
    acc = None
    r = x
    for p in range(passes):
        part = r.astype(BF16)
        d = _dot(part, w_bf)
        acc = d if acc is None else acc + d
        if p + 1 < passes:
            r = r - part.astype(F32)
    return acc


def _sigmoid(x):
    return 1.0 / (1.0 + jnp.exp2(x * -LOG2E))


def _silu(x):
    return x * _sigmoid(x)


def _causal_taps(acc, buf, w_ref, n_taps, pad):
    rows = buf.shape[0]
    for res in range(SUBLANES):
        shifted = None
        for j in range(n_taps):
            off = pad - (n_taps - 1) + j
            if off % SUBLANES != res:
                continue
            if shifted is None:
                shifted = buf if res == 0 else pltpu.roll(buf, rows - res, axis=0)
            base = off - res
            acc = acc + w_ref[j:j + 1, :] * shifted[base:base + TB]
    return acc


def _shift_rows(x, step, fill):
    if step % SUBLANES == 0:
        return jnp.concatenate([jnp.full((step, x.shape[1]), fill, F32), x[0:x.shape[0] - step]], axis=0)
    rolled = pltpu.roll(x, step, axis=0)
    row = lax.broadcasted_iota(jnp.int32, (SUBLANES, x.shape[1]), 0)
    head = jnp.where(row < step, fill, rolled[0:SUBLANES])
    return jnp.concatenate([head, rolled[SUBLANES:]], axis=0)


def _layer_kernel(layer,
                  x_ref, ng_ref, win_ref, wtail_ref, wout_ref, apw_ref, cwr_ref, cwi_ref, dwup_ref, aconv_ref,
                  acb_ref, alg_ref, alb_ref, ccw_ref, ccb_ref, cbr_ref, cbi_ref, lam_ref, dbu_ref,
                  dng_ref, bqg_ref, bkg_ref, bucket_ref, sinks_ref, relb_ref,
                  o_ref,
                  ubuf, cbuf, hcar, kbuf, vbuf, st, ycat, biasm, g256, tri, dwup, gates, pb, pd, plr):
    b_idx = pl.program_id(0)
    t_idx = pl.program_id(1)
    prow = lambda ref: ref[layer:layer + 1, :]

    @pl.when((b_idx == 0) & (t_idx == 0))
    def _build_tables():
        ri = lax.broadcasted_iota(jnp.int32, (W, W), 0) // HEAD
        ci = lax.broadcasted_iota(jnp.int32, (W, W), 1) // HEAD
        g256[...] = jnp.where(ri == ci, 1.0 / HEAD, 0.0).astype(BF16)
        rt = lax.broadcasted_iota(jnp.int32, (TB, TB), 0)
        ct = lax.broadcasted_iota(jnp.int32, (TB, TB), 1)
        tri[...] = jnp.where((rt // CHUNK == ct // CHUNK) & (rt >= ct), 1.0, 0.0).astype(BF16)
        dwup[...] = jnp.zeros((LANES, LANES), BF16)
        dwup[0:GATE_RANK, :] = dwup_ref[...].astype(BF16)
        bucket = bucket_ref[...]
        qi = lax.broadcasted_iota(jnp.int32, (QBLK, 2 * QBLK), 0)
        kj = lax.broadcasted_iota(jnp.int32, (QBLK, 2 * QBLK), 1)
        dist = qi + QBLK - kj
        valid = (dist >= 0) & (dist < WINDOW)
        for h in range(4):
            acc = jnp.zeros((QBLK, 2 * QBLK), F32)
            for bk in range(N_BUCKETS):
                acc = jnp.where(bucket == bk, relb_ref[bk * 4 + h], acc)
            biasm[h * QBLK:(h + 1) * QBLK, :] = jnp.where(valid, acc * LOG2E, -jnp.inf)

    @pl.when(t_idx == 0)
    def _reset_state():
        ubuf[0:A_PAD, :] = jnp.zeros((A_PAD, W), F32)
        cbuf[0:C_PAD, :] = jnp.zeros((C_PAD, W), F32)
        hcar[...] = jnp.zeros((SUBLANES, W), F32)
        kbuf[0:QBLK, :] = jnp.zeros((QBLK, LANES), BF16)
        vbuf[0:QBLK, :] = jnp.zeros((QBLK, LANES), BF16)
        st[...] = jnp.zeros((W, LANES), F32)

    x = x_ref[0]
    ms = jnp.mean(x * x, axis=-1, keepdims=True)
    hb = (x * lax.rsqrt(ms + EPS) * prow(ng_ref)).astype(BF16)

    gmat = g256[...]

    proj = lambda off, width: _dot(hb, win_ref[:, off:off + width])
    pav = proj(A_OFF, 2 * W)
    ubuf[A_PAD:A_PAD + TB, :] = pav[:, 0:W] * _sigmoid(pav[:, W:2 * W])
    gates[:, 0:W] = _silu(proj(A_OFF + 2 * W, W))
    pb[...] = proj(B_OFF, B_QKV)
    gates[:, W:2 * W] = _silu(proj(B_OFF + B_QKV, W))
    cbuf[C_PAD:C_PAD + TB, :] = proj(C_OFF, W)
    gates[:, 2 * W:3 * W] = _silu(proj(C_OFF + W, W))
    pd[...] = proj(D_OFF, D_QKV)
    ptail = _dot(hb, wtail_ref[...])
    gates[:, 3 * W:4 * W] = _silu(ptail[:, 0:W])
    plr[...] = ptail[:, W:W + LANES]

    conv = jnp.broadcast_to(prow(acb_ref), (TB, W))
    conv = _causal_taps(conv, ubuf[...], aconv_ref, CONV_A, A_PAD)
    ubuf[0:A_PAD, :] = ubuf[TB:TB + A_PAD, :]
    mu = _dot_split(conv, gmat, 2)
    dc = conv - mu
    var = _dot_split(dc * dc, gmat, 1)
    un = dc * lax.rsqrt(var + EPS) * prow(alg_ref) + prow(alb_ref)
    ya = _dot(_silu(un).astype(BF16), apw_ref[...]) * gates[:, 0:W]
    ycat[:, 0:W] = ya.astype(BF16)

    q = pb[:, 0:W]
    k = pb[:, W:W + LANES]
    v = pb[:, W + LANES:W + 2 * LANES]
    qn = q * lax.rsqrt(_dot_split(q * q, gmat, 1) + EPS)
    kn = k * lax.rsqrt(_dot_split(k * k, gmat[0:LANES, 0:LANES], 1) + EPS)
    kbuf[QBLK:QBLK + TB, :] = kn.astype(BF16)
    vbuf[QBLK:QBLK + TB, :] = v.astype(BF16)
    lane = lax.broadcasted_iota(jnp.int32, (1, LANES), 1)
    lanem = lane < HEAD
    qk_gain = prow(bqg_ref)[:, 0:LANES] * prow(bkg_ref) * (HEAD ** -0.5 * LOG2E)
    lo = jnp.where(lanem, qk_gain, 0.0)
    hi = jnp.where(lanem, 0.0, qk_gain)
    first_tile_mask = jnp.where(t_idx == 0, -jnp.inf, 0.0)

    def softmax_rows(sh, h, bi):
        if bi == 0:
            sh = jnp.concatenate([sh[:, 0:QBLK] + first_tile_mask, sh[:, QBLK:]], axis=1)
        sink = sinks_ref[layer * 4 + h] * LOG2E
        m = jnp.maximum(jnp.max(sh, axis=-1, keepdims=True), sink)
        p = jnp.exp2(sh - m)
        den = jnp.sum(p, axis=-1, keepdims=True) + jnp.exp2(sink - m)
        return p.astype(BF16), den

    for bi in range(TB // QBLK):
        r0 = bi * QBLK
        qa = qn[r0:r0 + QBLK, 0:LANES]
        qb = qn[r0:r0 + QBLK, LANES:2 * LANES]
        qst = jnp.concatenate([qa * lo, pltpu.roll(qa, HEAD, axis=1) * lo,
                               pltpu.roll(qb, HEAD, axis=1) * hi, qb * hi], axis=0).astype(BF16)
        s = _dot_nt(qst, kbuf[r0:r0 + 2 * QBLK, :]) + biasm[...]
        pden = [softmax_rows(s[h * QBLK:(h + 1) * QBLK, :], h, bi) for h in range(4)]
        ov = _dot(jnp.concatenate([t[0] for t in pden], axis=0), vbuf[r0:r0 + 2 * QBLK, :])
        oh = [ov[h * QBLK:(h + 1) * QBLK, :] / pden[h][1] for h in range(4)]
        y01 = jnp.where(lanem, oh[0], pltpu.roll(oh[1], HEAD, axis=1))
        y23 = jnp.where(lanem, pltpu.roll(oh[2], HEAD, axis=1), oh[3])
        ycat[r0:r0 + QBLK, W:2 * W] = (jnp.concatenate([y01, y23], axis=1)
                                       * gates[r0:r0 + QBLK, W:2 * W]).astype(BF16)
    kbuf[0:QBLK, :] = kbuf[TB:TB + QBLK, :]
    vbuf[0:QBLK, :] = vbuf[TB:TB + QBLK, :]

    xc = jnp.broadcast_to(prow(ccb_ref), (TB, W))
    xc = _causal_taps(xc, cbuf[...], ccw_ref, CONV_C, C_PAD)
    cbuf[0:C_PAD, :] = cbuf[TB:TB + C_PAD, :]
    xcb = xc.astype(BF16)
    rg = _sigmoid(_dot(xcb, cwr_ref[...]) + prow(cbr_ref))
    ig = _sigmoid(_dot(xcb, cwi_ref[...]) + prow(cbi_ref))
    nlam = -prow(lam_ref)
    softplus = jnp.maximum(nlam, 0.0) + jnp.log1p(jnp.exp(-jnp.abs(nlam)))
    a = jnp.exp2(rg * (softplus * (-LRU_C * LOG2E)))
    gap = 1.0 - a * a
    root = jnp.where(gap > 0.0, gap * lax.rsqrt(gap), 0.0)
    uu = root * (ig * xc)
    step = 1
    while step < TB:
        uu = a * _shift_rows(uu, step, 0.0) + uu
        a = a * _shift_rows(a, step, 1.0)
        step *= 2
    hs = uu + a * hcar[0:1, :]
    hcar[0:1, :] = hs[TB - 1:TB, :]
    ycat[:, 2 * W:3 * W] = (hs * gates[:, 2 * W:3 * W]).astype(BF16)

    dq = pd[:, 0:LANES] * (D_KEY ** -0.5)
    dk = pd[:, LANES:2 * LANES]
    dv = pd[:, 2 * LANES:2 * LANES + W]
    z = _dot(plr[...].astype(BF16), dwup[...]) + prow(dbu_ref)
    lg = (jnp.minimum(z, 0.0) - jnp.log1p(jnp.exp2(jnp.abs(z) * -LOG2E))) * (LOG2E / GATE_TAU)
    bcum = None
    r = lg
    for p_ in range(2):
        part = r.astype(BF16)
        d_ = _dot(tri[...], part)
        bcum = d_ if bcum is None else bcum + d_
        r = r - part.astype(F32)
    lane256 = lax.broadcasted_iota(jnp.int32, (1, W), 1)
    ci = lax.broadcasted_iota(jnp.int32, (CHUNK, W), 0)
    cj = lax.broadcasted_iota(jnp.int32, (CHUNK, W), 1) % CHUNK
    causal = (ci >= cj).astype(F32)
    bdm = (lax.broadcasted_iota(jnp.int32, (W, LANES), 0) // HEAD
           == lax.broadcasted_iota(jnp.int32, (W, LANES), 1) // D_KEY).astype(F32)
    state = st[...]
    od_parts = []
    for c in range(TB // CHUNK):
        r0 = c * CHUNK
        bc = bcum[r0:r0 + CHUNK, :]
        bl = bc[CHUNK - 1:CHUNK, :]
        qt = (dq[r0:r0 + CHUNK, :] * jnp.exp2(bc)).astype(BF16)
        kc = dk[r0:r0 + CHUNK, :]
        kt = kc * jnp.exp2(-bc)
        ke = (kc * jnp.exp2(bl - bc)).astype(BF16)
        vc = dv[r0:r0 + CHUNK, :]
        kst = jnp.concatenate([kt * (lane // D_KEY == h).astype(F32) for h in range(D_HEADS)],
                              axis=0).astype(BF16)
        att = (_dot_nt(qt, kst) * causal).astype(BF16)
        vbd = jnp.concatenate([vc * (lane256 // HEAD == h).astype(F32) for h in range(D_HEADS)],
                              axis=0).astype(BF16)
        o_c = _dot(att, vbd) + _dot_nt(qt, state.astype(BF16))
        state = state * jnp.exp2(bl) + _dot_tn(vc.astype(BF16), ke) * bdm
        od_parts.append(o_c)
    st[...] = state
    od = jnp.concatenate(od_parts, axis=0)
    odn = od * lax.rsqrt(_dot_split(od * od, gmat, 1) + EPS) * prow(dng_ref)
    ycat[:, 3 * W:4 * W] = (odn * gates[:, 3 * W:4 * W]).astype(BF16)

    o_ref[0] = x + _dot(ycat[...], wout_ref[...])


def _t5_bucket(dist):
    max_exact = N_BUCKETS // 2
    d = jnp.maximum(dist, 1).astype(F32)
    large = max_exact + (jnp.log(d / max_exact) / math.log(MAX_DISTANCE / max_exact)
                         * (N_BUCKETS - max_exact)).astype(jnp.int32)
    large = jnp.minimum(large, N_BUCKETS - 1)
    return jnp.where(dist < max_exact, dist, large)


def _layer_spec(layer, shape):
    return pl.BlockSpec((None,) + shape, lambda b, t: (layer,) + (0,) * len(shape))


def _whole(shape):
    return pl.BlockSpec(shape, lambda b, t: (0,) * len(shape))


def _layer_call(layer, x, ng, win, wtail, wout, apw, cwr, cwi, dwup, aconv, vec256, ccw, dbu, dng, bqg, bkg,
                bucket, sinks, relb):
    bsz, seq, _ = x.shape
    smem = pl.BlockSpec(memory_space=pltpu.SMEM)
    spec = functools.partial(_layer_spec, layer)
    acb, alg, alb, ccb, cbr, cbi, lam = vec256
    return pl.pallas_call(
        functools.partial(_layer_kernel, layer),
        grid=(bsz, seq // TB),
        in_specs=[
            pl.BlockSpec((1, TB, D_MODEL), lambda b, t: (b, t, 0)),
            _whole((DEPTH, D_MODEL)), spec((D_MODEL, IN_COLS)), spec((D_MODEL, D_TAIL)),
            spec((D_MODEL, D_MODEL)), spec((W, W)), spec((W, W)), spec((W, W)),
            spec((GATE_RANK, LANES)), spec((CONV_A, W)),
            _whole((DEPTH, W)), _whole((DEPTH, W)), _whole((DEPTH, W)),
            spec((CONV_C, W)),
            _whole((DEPTH, W)), _whole((DEPTH, W)), _whole((DEPTH, W)), _whole((DEPTH, W)),
            _whole((DEPTH, LANES)),
            _whole((DEPTH, W)), _whole((DEPTH, W)), _whole((DEPTH, LANES)),
            _whole((QBLK, 2 * QBLK)),
            smem, smem,
        ],
        out_specs=pl.BlockSpec((1, TB, D_MODEL), lambda b, t: (b, t, 0)),
        out_shape=jax.ShapeDtypeStruct(x.shape, F32),
        scratch_shapes=[
            pltpu.VMEM((TB + A_PAD, W), F32),
            pltpu.VMEM((TB + C_PAD, W), F32),
            pltpu.VMEM((SUBLANES, W), F32),
            pltpu.VMEM((TB + QBLK, LANES), BF16),
            pltpu.VMEM((TB + QBLK, LANES), BF16),
            pltpu.VMEM((W, LANES), F32),
            pltpu.VMEM((TB, D_MODEL), BF16),
            pltpu.VMEM((4 * QBLK, 2 * QBLK), F32),
            pltpu.VMEM((W, W), BF16),
            pltpu.VMEM((TB, TB), BF16),
            pltpu.VMEM((LANES, LANES), BF16),
            pltpu.VMEM((TB, D_MODEL), F32),
            pltpu.VMEM((TB, B_QKV), F32),
            pltpu.VMEM((TB, D_QKV), F32),
            pltpu.VMEM((TB, LANES), F32),
        ],
        compiler_params=pltpu.CompilerParams(
            dimension_semantics=("arbitrary", "arbitrary"),
            vmem_limit_bytes=VMEM_LIMIT_BYTES,
        ),
        name="hybrid_layer",
    )(x, ng, win, wtail, wout, apw, cwr, cwi, dwup, aconv, acb, alg, alb, ccw, ccb, cbr, cbi, lam, dbu,
      dng, bqg, bkg, bucket, sinks, relb)


def _block_diag(blocks):
    depth, n, c, _ = blocks.shape
    eye = jnp.eye(n, dtype=blocks.dtype)
    return (eye[None, :, None, :, None] * blocks[:, :, :, None, :]).reshape(depth, n * c, n * c)


def kernel(x, norm_g, w_in, a_conv_w, a_conv_b, a_ln_g, a_ln_b, a_pw, b_q_g, b_k_g, b_sinks, rel_bias,
           c_conv_w, c_conv_b, c_w_r, c_b_r, c_w_i, c_b_i, c_lambda, d_w_up, d_b_up, d_norm_g, w_out):
    dist = jnp.arange(QBLK)[:, None] + QBLK - jnp.arange(2 * QBLK)[None, :]
    bucket = _t5_bucket(jnp.clip(dist, 0, None)).astype(jnp.int32)
    wtail = jnp.concatenate([
        w_in[:, :, D_TAIL_SRC + GATE_RANK:IN_COLS], w_in[:, :, D_TAIL_SRC:D_TAIL_SRC + GATE_RANK],
        jnp.zeros((DEPTH, D_MODEL, D_TAIL - (IN_COLS - D_TAIL_SRC)), F32)], axis=2).astype(BF16)
    vec256 = (a_conv_b, a_ln_g, a_ln_b, c_conv_b, c_b_r, c_b_i, c_lambda)
    args = (norm_g, w_in.astype(BF16), wtail, w_out.astype(BF16), a_pw.astype(BF16),
            _block_diag(c_w_r).astype(BF16), _block_diag(c_w_i).astype(BF16), d_w_up, a_conv_w,
            vec256, c_conv_w, d_b_up, jnp.tile(d_norm_g, (1, 4)), jnp.tile(b_q_g, (1, 4)),
            jnp.tile(b_k_g, (1, 2)), bucket, b_sinks.reshape(-1), rel_bias.reshape(-1))
    for l in range(DEPTH):
        x = _layer_call(l, x, *args)
    return x
```

```python
import functools
import math

import jax
import jax.numpy as jnp
from jax import lax
from jax.experimental import pallas as pl
from jax.experimental.pallas import tpu as pltpu

F32 = jnp.float32
BF16 = jnp.bfloat16

D_MODEL = 1024
DEPTH = 4
W = 256
EPS = 1e-6
CONV_A = 31
A_PAD = 32
HEAD = 64
WINDOW = 128
QBLK = 128
N_BUCKETS = 32
MAX_DISTANCE = 128
CONV_C = 4
C_PAD = 8
LRU_C = 8.0
D_HEADS = 4
D_KEY = 32
GATE_RANK = 16
GATE_TAU = 16.0
CHUNK = 64
SUBLANES = 8
LANES = 128
VMEM_LIMIT_BYTES = 48 * 1024 * 1024
LOG2E = math.log2(math.e)

TB = 256

A_OFF = 0
B_OFF = 768
B_QKV = 512
C_OFF = 1536
D_OFF, D_QKV = 2048, 512
D_TAIL_SRC = D_OFF + D_QKV
D_TAIL = 384
IN_COLS = 2832


def _dot(a, b):
    return jnp.dot(a, b, preferred_element_type=F32)


def _dot_nt(a, b):
    return lax.dot_general(a, b, (((1,), (1,)), ((), ())), preferred_element_type=F32)


def _dot_tn(a, b):
    return lax.dot_general(a, b, (((0,), (0,)), ((), ())), preferred_element_type=F32)


def _dot_split(x, w_bf, passes):
    acc = None
    r = x
    for p in range(passes):
        part = r.astype(BF16)
        d = _dot(part, w_bf)
        acc = d if acc is None else acc + d
        if p + 1 < passes:
            r = r - part.astype(F32)
    return acc


def _sigmoid(x):
    return 1.0 / (1.0 + jnp.exp2(x * -LOG2E))


def _silu(x):
    return x * _sigmoid(x)


def _causal_taps(acc, buf, w_ref, n_taps, pad):
    rows = buf.shape[0]
    for res in range(SUBLANES):
        shifted = None
        for j in range(n_taps):
            off = pad - (n_taps - 1) + j
            if off % SUBLANES != res:
                continue
            if shifted is None:
                shifted = buf if res == 0 else pltpu.roll(buf, rows - res, axis=0)
            base = off - res
            acc = acc + w_ref[j:j + 1, :] * shifted[base:base + TB]
    return acc


def _shift_rows(x, step, fill):
    if step % SUBLANES == 0:
        return jnp.concatenate([jnp.full((step, x.shape[1]), fill, F32), x[0:x.shape[0] - step]], axis=0)
    rolled = pltpu.roll(x, step, axis=0)
    row = lax.broadcasted_iota(jnp.int32, (SUBLANES, x.shape[1]), 0)
    head = jnp.where(row < step, fill, rolled[0:SUBLANES])
    return jnp.concatenate([head, rolled[SUBLANES:]], axis=0)


def _layer_kernel(layer, tiles_per_seq,
                  x_ref, xo_ref, ng_ref, win_ref, wtail_ref, wout_ref, apw_ref, cwr_ref, cwi_ref, dwup_ref, aconv_ref,
                  acb_ref, alg_ref, alb_ref, ccw_ref, ccb_ref, cbr_ref, cbi_ref, lam_ref, dbu_ref,
                  dng_ref, bqg_ref, bkg_ref, bucket_ref, sinks_ref, relb_ref,
                  o_ref,
                  ubuf, cbuf, hcar, kbuf, vbuf, st, ycat, biasm, g256, tri, dwup, gates, pb, pd, plr, yprev):
    s_idx = pl.program_id(0)
    seq_start = (s_idx % tiles_per_seq) == 0
    prow = lambda ref: ref[layer:layer + 1, :]

    @pl.when(s_idx == 0)
    def _build_tables():
        yprev[...] = jnp.zeros((TB, D_MODEL), BF16)
        ri = lax.broadcasted_iota(jnp.int32, (W, W), 0) // HEAD
        ci = lax.broadcasted_iota(jnp.int32, (W, W), 1) // HEAD
        g256[...] = jnp.where(ri == ci, 1.0 / HEAD, 0.0).astype(BF16)
        rt = lax.broadcasted_iota(jnp.int32, (TB, TB), 0)
        ct = lax.broadcasted_iota(jnp.int32, (TB, TB), 1)
        tri[...] = jnp.where((rt // CHUNK == ct // CHUNK) & (rt >= ct), 1.0, 0.0).astype(BF16)
        dwup[...] = jnp.zeros((LANES, LANES), BF16)
        dwup[0:GATE_RANK, :] = dwup_ref[...].astype(BF16)
        bucket = bucket_ref[...]
        qi = lax.broadcasted_iota(jnp.int32, (QBLK, 2 * QBLK), 0)
        kj = lax.broadcasted_iota(jnp.int32, (QBLK, 2 * QBLK), 1)
        dist = qi + QBLK - kj
        valid = (dist >= 0) & (dist < WINDOW)
        for h in range(4):
            acc = jnp.zeros((QBLK, 2 * QBLK), F32)
            for bk in range(N_BUCKETS):
                acc = jnp.where(bucket == bk, relb_ref[bk * 4 + h], acc)
            biasm[h * QBLK:(h + 1) * QBLK, :] = jnp.where(valid, acc * LOG2E, -jnp.inf)

    @pl.when(seq_start)
    def _reset_state():
        ubuf[0:A_PAD, :] = jnp.zeros((A_PAD, W), F32)
        cbuf[0:C_PAD, :] = jnp.zeros((C_PAD, W), F32)
        hcar[...] = jnp.zeros((SUBLANES, W), F32)
        kbuf[0:QBLK, :] = jnp.zeros((QBLK, LANES), BF16)
        vbuf[0:QBLK, :] = jnp.zeros((QBLK, LANES), BF16)
        st[...] = jnp.zeros((W, LANES), F32)

    x = x_ref[0]
    ms = jnp.mean(x * x, axis=-1, keepdims=True)
    hb = (x * lax.rsqrt(ms + EPS) * prow(ng_ref)).astype(BF16)

    gmat = g256[...]

    proj = lambda off, width: _dot(hb, win_ref[:, off:off + width])
    pav = proj(A_OFF, 2 * W)
    ubuf[A_PAD:A_PAD + TB, :] = pav[:, 0:W] * _sigmoid(pav[:, W:2 * W])

    def project_rest():
        gates[:, 0:W] = _silu(proj(A_OFF + 2 * W, W))
        pb[...] = proj(B_OFF, B_QKV)
        gates[:, W:2 * W] = _silu(proj(B_OFF + B_QKV, W))
        cbuf[C_PAD:C_PAD + TB, :] = proj(C_OFF, W)
        gates[:, 2 * W:3 * W] = _silu(proj(C_OFF + W, W))
        pd[...] = proj(D_OFF, D_QKV)
        ptail = _dot(hb, wtail_ref[...])
        gates[:, 3 * W:4 * W] = _silu(ptail[:, 0:W])
        plr[...] = ptail[:, W:W + LANES]


    def mixer_a():
        conv = jnp.broadcast_to(prow(acb_ref), (TB, W))
        conv = _causal_taps(conv, ubuf[...], aconv_ref, CONV_A, A_PAD)
        ubuf[0:A_PAD, :] = ubuf[TB:TB + A_PAD, :]
        yield
        mu = _dot_split(conv, gmat, 2)
        yield
        dc = conv - mu
        var = _dot_split(dc * dc, gmat, 1)
        yield
        un = dc * lax.rsqrt(var + EPS) * prow(alg_ref) + prow(alb_ref)
        ya = _dot(_silu(un).astype(BF16), apw_ref[...]) * gates[:, 0:W]
        ycat[:, 0:W] = ya.astype(BF16)

    def mixer_b():
        q = pb[:, 0:W]
        k = pb[:, W:W + LANES]
        v = pb[:, W + LANES:W + 2 * LANES]
        qn = q * lax.rsqrt(_dot_split(q * q, gmat, 1) + EPS)
        kn = k * lax.rsqrt(_dot_split(k * k, gmat[0:LANES, 0:LANES], 1) + EPS)
        yield
        kbuf[QBLK:QBLK + TB, :] = kn.astype(BF16)
        vbuf[QBLK:QBLK + TB, :] = v.astype(BF16)
        lane = lax.broadcasted_iota(jnp.int32, (1, LANES), 1)
        lanem = lane < HEAD
        qk_gain = prow(bqg_ref)[:, 0:LANES] * prow(bkg_ref) * (HEAD ** -0.5 * LOG2E)
        lo = jnp.where(lanem, qk_gain, 0.0)
        hi = jnp.where(lanem, 0.0, qk_gain)
        first_tile_mask = jnp.where(seq_start, -jnp.inf, 0.0)

        def softmax_rows(sh, h, bi):
            if bi == 0:
                sh = jnp.concatenate([sh[:, 0:QBLK] + first_tile_mask, sh[:, QBLK:]], axis=1)
            sink = sinks_ref[layer * 4 + h] * LOG2E
            m = jnp.maximum(jnp.max(sh, axis=-1, keepdims=True), sink)
            p = jnp.exp2(sh - m)
            den = jnp.sum(p, axis=-1, keepdims=True) + jnp.exp2(sink - m)
            return p.astype(BF16), den

        for bi in range(TB // QBLK):
            r0 = bi * QBLK
            qa = qn[r0:r0 + QBLK, 0:LANES]
            qb = qn[r0:r0 + QBLK, LANES:2 * LANES]
            qst = jnp.concatenate([qa * lo, pltpu.roll(qa, HEAD, axis=1) * lo,
                                   pltpu.roll(qb, HEAD, axis=1) * hi, qb * hi], axis=0).astype(BF16)
            s = _dot_nt(qst, kbuf[r0:r0 + 2 * QBLK, :]) + biasm[...]
            yield
            pden = [softmax_rows(s[h * QBLK:(h + 1) * QBLK, :], h, bi) for h in range(4)]
            ov = _dot(jnp.concatenate([t[0] for t in pden], axis=0), vbuf[r0:r0 + 2 * QBLK, :])
            yield
            oh = [ov[h * QBLK:(h + 1) * QBLK, :] / pden[h][1] for h in range(4)]
            y01 = jnp.where(lanem, oh[0], pltpu.roll(oh[1], HEAD, axis=1))
            y23 = jnp.where(lanem, pltpu.roll(oh[2], HEAD, axis=1), oh[3])
            ycat[r0:r0 + QBLK, W:2 * W] = (jnp.concatenate([y01, y23], axis=1)
                                           * gates[r0:r0 + QBLK, W:2 * W]).astype(BF16)
        kbuf[0:QBLK, :] = kbuf[TB:TB + QBLK, :]
        vbuf[0:QBLK, :] = vbuf[TB:TB + QBLK, :]

    def mixer_c():
        xc = jnp.broadcast_to(prow(ccb_ref), (TB, W))
        xc = _causal_taps(xc, cbuf[...], ccw_ref, CONV_C, C_PAD)
        cbuf[0:C_PAD, :] = cbuf[TB:TB + C_PAD, :]
        xcb = xc.astype(BF16)
        rg = _sigmoid(_dot(xcb, cwr_ref[...]) + prow(cbr_ref))
        ig = _sigmoid(_dot(xcb, cwi_ref[...]) + prow(cbi_ref))
        yield
        nlam = -prow(lam_ref)
        softplus = jnp.maximum(nlam, 0.0) + jnp.log1p(jnp.exp(-jnp.abs(nlam)))
        a = jnp.exp2(rg * (softplus * (-LRU_C * LOG2E)))
        gap = 1.0 - a * a
        root = jnp.where(gap > 0.0, gap * lax.rsqrt(gap), 0.0)
        uu = root * (ig * xc)
        step = 1
        while step < TB:
            uu = a * _shift_rows(uu, step, 0.0) + uu
            a = a * _shift_rows(a, step, 1.0)
            step *= 2
        hs = uu + a * hcar[0:1, :]
        hcar[0:1, :] = hs[TB - 1:TB, :]
        ycat[:, 2 * W:3 * W] = (hs * gates[:, 2 * W:3 * W]).astype(BF16)

    def mixer_d():
        dq = pd[:, 0:LANES] * (D_KEY ** -0.5)
        dk = pd[:, LANES:2 * LANES]
        dv = pd[:, 2 * LANES:2 * LANES + W]
        z = _dot(plr[...].astype(BF16), dwup[...]) + prow(dbu_ref)
        yield
        lg = (jnp.minimum(z, 0.0) - jnp.log1p(jnp.exp2(jnp.abs(z) * -LOG2E))) * (LOG2E / GATE_TAU)
        bcum = None
        r = lg
        for p_ in range(2):
            part = r.astype(BF16)
            d_ = _dot(tri[...], part)
            bcum = d_ if bcum is None else bcum + d_
            r = r - part.astype(F32)
        yield
        lane = lax.broadcasted_iota(jnp.int32, (1, LANES), 1)
        lane256 = lax.broadcasted_iota(jnp.int32, (1, W), 1)
        ci = lax.broadcasted_iota(jnp.int32, (CHUNK, W), 0)
        cj = lax.broadcasted_iota(jnp.int32, (CHUNK, W), 1) % CHUNK
        causal = (ci >= cj).astype(F32)
        bdm = (lax.broadcasted_iota(jnp.int32, (W, LANES), 0) // HEAD
               == lax.broadcasted_iota(jnp.int32, (W, LANES), 1) // D_KEY).astype(F32)
        state = st[...]
        od_parts = []
        for c in range(TB // CHUNK):
            r0 = c * CHUNK
            bc = bcum[r0:r0 + CHUNK, :]
            bl = bc[CHUNK - 1:CHUNK, :]
            qt = (dq[r0:r0 + CHUNK, :] * jnp.exp2(bc)).astype(BF16)
            kc = dk[r0:r0 + CHUNK, :]
            kt = kc * jnp.exp2(-bc)
            ke = (kc * jnp.exp2(bl - bc)).astype(BF16)
            vc = dv[r0:r0 + CHUNK, :]
            kst = jnp.concatenate([kt * (lane // D_KEY == h).astype(F32) for h in range(D_HEADS)],
                                  axis=0).astype(BF16)
            att = (_dot_nt(qt, kst) * causal).astype(BF16)
            yield
            vbd = jnp.concatenate([vc * (lane256 // HEAD == h).astype(F32) for h in range(D_HEADS)],
                                  axis=0).astype(BF16)
            o_c = _dot(att, vbd) + _dot_nt(qt, state.astype(BF16))
            state = state * jnp.exp2(bl) + _dot_tn(vc.astype(BF16), ke) * bdm
            od_parts.append(o_c)
            yield
        st[...] = state
        od = jnp.concatenate(od_parts, axis=0)
        odn = od * lax.rsqrt(_dot_split(od * od, gmat, 1) + EPS) * prow(dng_ref)
        ycat[:, 3 * W:4 * W] = (odn * gates[:, 3 * W:4 * W]).astype(BF16)

    conv_first = mixer_a()
    next(conv_first)
    project_rest()
    def out_proj_prev():
        for c0 in range(0, D_MODEL, W):
            yield
            yield
            o_ref[0, :, c0:c0 + W] = xo_ref[0, :, c0:c0 + W] + _dot(yprev[...], wout_ref[:, c0:c0 + W])

    running = [mixer_d(), conv_first, mixer_b(), mixer_c(), out_proj_prev()]
    while running:
        running = [g for g in running if next(g, StopIteration) is not StopIteration]
    yprev[...] = ycat[...]


def _t5_bucket(dist):
    max_exact = N_BUCKETS // 2
    d = jnp.maximum(dist, 1).astype(F32)
    large = max_exact + (jnp.log(d / max_exact) / math.log(MAX_DISTANCE / max_exact)
                         * (N_BUCKETS - max_exact)).astype(jnp.int32)
    large = jnp.minimum(large, N_BUCKETS - 1)
    return jnp.where(dist < max_exact, dist, large)


def _layer_spec(layer, shape):
    return pl.BlockSpec((None,) + shape, lambda s: (layer,) + (0,) * len(shape))


def _whole(shape):
    return pl.BlockSpec(shape, lambda s: (0,) * len(shape))


def _layer_call(layer, tiles_per_seq, x, ng, win, wtail, wout, apw, cwr, cwi, dwup, aconv, vec256, ccw, dbu,
                dng, bqg, bkg, bucket, sinks, relb):
    n_tiles = x.shape[0]
    smem = pl.BlockSpec(memory_space=pltpu.SMEM)
    spec = functools.partial(_layer_spec, layer)
    acb, alg, alb, ccb, cbr, cbi, lam = vec256
    prev_tile = lambda s: (jnp.maximum(s - 1, 0), 0, 0)
    return pl.pallas_call(
        functools.partial(_layer_kernel, layer, tiles_per_seq),
        grid=(n_tiles + 1,),
        in_specs=[
            pl.BlockSpec((1, TB, D_MODEL), lambda s: (jnp.minimum(s, n_tiles - 1), 0, 0)),
            pl.BlockSpec((1, TB, D_MODEL), prev_tile),
            _whole((DEPTH, D_MODEL)), spec((D_MODEL, IN_COLS)), spec((D_MODEL, D_TAIL)),
            spec((D_MODEL, D_MODEL)), spec((W, W)), spec((W, W)), spec((W, W)),
            spec((GATE_RANK, LANES)), spec((CONV_A, W)),
            _whole((DEPTH, W)), _whole((DEPTH, W)), _whole((DEPTH, W)),
            spec((CONV_C, W)),
            _whole((DEPTH, W)), _whole((DEPTH, W)), _whole((DEPTH, W)), _whole((DEPTH, W)),
            _whole((DEPTH, LANES)),
            _whole((DEPTH, W)), _whole((DEPTH, W)), _whole((DEPTH, LANES)),
            _whole((QBLK, 2 * QBLK)),
            smem, smem,
        ],
        out_specs=pl.BlockSpec((1, TB, D_MODEL), prev_tile),
        out_shape=jax.ShapeDtypeStruct(x.shape, F32),
        scratch_shapes=[
            pltpu.VMEM((TB + A_PAD, W), F32),
            pltpu.VMEM((TB + C_PAD, W), F32),
            pltpu.VMEM((SUBLANES, W), F32),
            pltpu.VMEM((TB + QBLK, LANES), BF16),
            pltpu.VMEM((TB + QBLK, LANES), BF16),
            pltpu.VMEM((W, LANES), F32),
            pltpu.VMEM((TB, D_MODEL), BF16),
            pltpu.VMEM((4 * QBLK, 2 * QBLK), F32),
            pltpu.VMEM((W, W), BF16),
            pltpu.VMEM((TB, TB), BF16),
            pltpu.VMEM((LANES, LANES), BF16),
            pltpu.VMEM((TB, D_MODEL), F32),
            pltpu.VMEM((TB, B_QKV), F32),
            pltpu.VMEM((TB, D_QKV), F32),
            pltpu.VMEM((TB, LANES), F32),
            pltpu.VMEM((TB, D_MODEL), BF16),
        ],
        compiler_params=pltpu.CompilerParams(
            dimension_semantics=("arbitrary",),
            vmem_limit_bytes=VMEM_LIMIT_BYTES,
        ),
        name="hybrid_layer",
    )(x, x, ng, win, wtail, wout, apw, cwr, cwi, dwup, aconv, acb, alg, alb, ccw, ccb, cbr, cbi, lam, dbu,
      dng, bqg, bkg, bucket, sinks, relb)


def _block_diag(blocks):
    depth, n, c, _ = blocks.shape
    eye = jnp.eye(n, dtype=blocks.dtype)
    return (eye[None, :, None, :, None] * blocks[:, :, :, None, :]).reshape(depth, n * c, n * c)


def kernel(x, norm_g, w_in, a_conv_w, a_conv_b, a_ln_g, a_ln_b, a_pw, b_q_g, b_k_g, b_sinks, rel_bias,
           c_conv_w, c_conv_b, c_w_r, c_b_r, c_w_i, c_b_i, c_lambda, d_w_up, d_b_up, d_norm_g, w_out):
    dist = jnp.arange(QBLK)[:, None] + QBLK - jnp.arange(2 * QBLK)[None, :]
    bucket = _t5_bucket(jnp.clip(dist, 0, None)).astype(jnp.int32)
    wtail = jnp.concatenate([
        w_in[:, :, D_TAIL_SRC + GATE_RANK:IN_COLS], w_in[:, :, D_TAIL_SRC:D_TAIL_SRC + GATE_RANK],
        jnp.zeros((DEPTH, D_MODEL, D_TAIL - (IN_COLS - D_TAIL_SRC)), F32)], axis=2).astype(BF16)
    vec256 = (a_conv_b, a_ln_g, a_ln_b, c_conv_b, c_b_r, c_b_i, c_lambda)
    args = (norm_g, w_in.astype(BF16), wtail, w_out.astype(BF16), a_pw.astype(BF16),
            _block_diag(c_w_r).astype(BF16), _block_diag(c_w_i).astype(BF16), d_w_up, a_conv_w,
            vec256, c_conv_w, d_b_up, jnp.tile(d_norm_g, (1, 4)), jnp.tile(b_q_g, (1, 4)),
            jnp.tile(b_k_g, (1, 2)), bucket, b_sinks.reshape(-1), rel_bias.reshape(-1))
    bsz, seq, _ = x.shape
    assert seq % TB == 0
    h = x.reshape(bsz * seq // TB, TB, D_MODEL)
    for l in range(DEPTH):
        h = _layer_call(l, seq // TB, h, *args)
    return h.reshape(bsz, seq, D_MODEL)
```

```python
import functools
import math

import jax
import jax.numpy as jnp
from jax import lax
from jax.experimental import pallas as pl
from jax.experimental.pallas import tpu as pltpu

F32 = jnp.float32
BF16 = jnp.bfloat16

D_MODEL = 1024
DEPTH = 4
W = 256
EPS = 1e-6
CONV_A = 31
A_PAD = 32
HEAD = 64
WINDOW = 128
QBLK = 128
N_BUCKETS = 32
MAX_DISTANCE = 128
CONV_C = 4
C_PAD = 8
LRU_C = 8.0
D_HEADS = 4
D_KEY = 32
GATE_RANK = 16
GATE_TAU = 16.0
CHUNK = 64
SUBLANES = 8
LANES = 128
VMEM_LIMIT_BYTES = 48 * 1024 * 1024
LOG2E = math.log2(math.e)

TB = 256
TRACE_ORDER = "O P A P P C P P B D P C O D B P D D D D D D D D B A O P B B P O A A D B".split()
OUT_SLAB = 256

A_OFF = 0
B_OFF = 768
B_QKV = 512
C_OFF = 1536
D_OFF, D_QKV = 2048, 512
D_TAIL_SRC = D_OFF + D_QKV
D_TAIL = 384
IN_COLS = 2832


def _dot(a, b):
    return jnp.dot(a, b, preferred_element_type=F32)


def _dot_nt(a, b):
    return lax.dot_general(a, b, (((1,), (1,)), ((), ())), preferred_element_type=F32)


def _dot_tn(a, b):
    return lax.dot_general(a, b, (((0,), (0,)), ((), ())), preferred_element_type=F32)


def _dot_split(x, w_bf, passes):
    acc = None
    r = x
    for p in range(passes):
        part = r.astype(BF16)
        d = _dot(part, w_bf)
        acc = d if acc is None else acc + d
        if p + 1 < passes:
            r = r - part.astype(F32)
    return acc


def _sigmoid(x):
    return 1.0 / (1.0 + jnp.exp2(x * -LOG2E))


def _silu(x):
    return x * _sigmoid(x)


def _causal_taps(acc, buf, w_ref, n_taps, pad):
    rows = buf.shape[0]
    for res in range(SUBLANES):
        shifted = None
        for j in range(n_taps):
            off = pad - (n_taps - 1) + j
            if off % SUBLANES != res:
                continue
            if shifted is None:
                shifted = buf if res == 0 else pltpu.roll(buf, rows - res, axis=0)
            base = off - res
            acc = acc + w_ref[j:j + 1, :] * shifted[base:base + TB]
    return acc


def _shift_rows(x, step, fill):
    if step % SUBLANES == 0:
        return jnp.concatenate([jnp.full((step, x.shape[1]), fill, F32), x[0:x.shape[0] - step]], axis=0)
    rolled = pltpu.roll(x, step, axis=0)
    row = lax.broadcasted_iota(jnp.int32, (SUBLANES, x.shape[1]), 0)
    head = jnp.where(row < step, fill, rolled[0:SUBLANES])
    return jnp.concatenate([head, rolled[SUBLANES:]], axis=0)


def _layer_kernel(layer, tiles_per_seq, n_tiles,
                  x_ref, xo_ref, ng_ref, win_ref, wtail_ref, wout_ref, apw_ref, cwr_ref, cwi_ref, dwup_ref, aconv_ref,
                  acb_ref, alg_ref, alb_ref, ccw_ref, ccb_ref, cbr_ref, cbi_ref, lam_ref, dbu_ref,
                  dng_ref, bqg_ref, bkg_ref, bucket_ref, sinks_ref, relb_ref,
                  o_ref,
                  ubuf, cbuf, hcar, kbuf, vbuf, st, ycat, biasm, g256, tri, dwup, gates, pb, pd, plr, yprev):
    s_idx = pl.program_id(0)
    seq_start = (s_idx % tiles_per_seq) == 0
    prow = lambda ref: ref[layer:layer + 1, :]

    @pl.when(s_idx == 0)
    def _build_tables():
        yprev[...] = jnp.zeros((TB, D_MODEL), BF16)
        ri = lax.broadcasted_iota(jnp.int32, (W, W), 0) // HEAD
        ci = lax.broadcasted_iota(jnp.int32, (W, W), 1) // HEAD
        g256[...] = jnp.where(ri == ci, 1.0 / HEAD, 0.0).astype(BF16)
        rt = lax.broadcasted_iota(jnp.int32, (TB, TB), 0)
        ct = lax.broadcasted_iota(jnp.int32, (TB, TB), 1)
        tri[...] = jnp.where((rt // CHUNK == ct // CHUNK) & (rt >= ct), 1.0, 0.0).astype(BF16)
        dwup[...] = jnp.zeros((LANES, LANES), BF16)
        dwup[0:GATE_RANK, :] = dwup_ref[...].astype(BF16)
        bucket = bucket_ref[...]
        qi = lax.broadcasted_iota(jnp.int32, (QBLK, 2 * QBLK), 0)
        kj = lax.broadcasted_iota(jnp.int32, (QBLK, 2 * QBLK), 1)
        dist = qi + QBLK - kj
        valid = (dist >= 0) & (dist < WINDOW)
        for h in range(4):
            acc = jnp.zeros((QBLK, 2 * QBLK), F32)
            for bk in range(N_BUCKETS):
                acc = jnp.where(bucket == bk, relb_ref[bk * 4 + h], acc)
            biasm[h * QBLK:(h + 1) * QBLK, :] = jnp.where(valid, acc * LOG2E, -jnp.inf)

    @pl.when(seq_start)
    def _reset_state():
        ubuf[0:A_PAD, :] = jnp.zeros((A_PAD, W), F32)
        cbuf[0:C_PAD, :] = jnp.zeros((C_PAD, W), F32)
        hcar[...] = jnp.zeros((SUBLANES, W), F32)
        kbuf[0:QBLK, :] = jnp.zeros((QBLK, LANES), BF16)
        vbuf[0:QBLK, :] = jnp.zeros((QBLK, LANES), BF16)
        st[...] = jnp.zeros((W, LANES), F32)

    hb = gmat = None

    proj = lambda off, width: _dot(hb, win_ref[:, off:off + width])
    shared = {}

    def project():
        pav = proj(A_OFF, 2 * W)
        ubuf[A_PAD:A_PAD + TB, :] = pav[:, 0:W] * _sigmoid(pav[:, W:2 * W])
        yield
        cbuf[C_PAD:C_PAD + TB, :] = proj(C_OFF, W)
        yield
        pb[...] = proj(B_OFF, B_QKV)
        yield
        pd[...] = proj(D_OFF, D_QKV)
        yield
        plr[...] = _dot(hb, wtail_ref[:, W:W + LANES])
        yield
        gates[:, 2 * W:3 * W] = _silu(proj(C_OFF + W, W))
        yield
        gates[:, 0:W] = _silu(proj(A_OFF + 2 * W, W))
        yield
        gates[:, W:2 * W] = _silu(proj(B_OFF + B_QKV, W))
        yield
        gates[:, 3 * W:4 * W] = _silu(_dot(hb, wtail_ref[:, 0:W]))


    def mixer_a():
        conv = jnp.broadcast_to(prow(acb_ref), (TB, W))
        conv = _causal_taps(conv, ubuf[...], aconv_ref, CONV_A, A_PAD)
        ubuf[0:A_PAD, :] = ubuf[TB:TB + A_PAD, :]
        yield
        mu = _dot_split(conv, gmat, 2)
        yield
        dc = conv - mu
        var = _dot_split(dc * dc, gmat, 1)
        yield
        un = dc * lax.rsqrt(var + EPS) * prow(alg_ref) + prow(alb_ref)
        ya = _dot(_silu(un).astype(BF16), apw_ref[...]) * gates[:, 0:W]
        ycat[:, 0:W] = ya.astype(BF16)

    def mixer_b():
        q = pb[:, 0:W]
        k = pb[:, W:W + LANES]
        v = pb[:, W + LANES:W + 2 * LANES]
        qn = q * lax.rsqrt(_dot_split(q * q, gmat, 1) + EPS)
        kn = k * lax.rsqrt(_dot_split(k * k, gmat[0:LANES, 0:LANES], 1) + EPS)
        yield
        kbuf[QBLK:QBLK + TB, :] = kn.astype(BF16)
        vbuf[QBLK:QBLK + TB, :] = v.astype(BF16)
        lane = lax.broadcasted_iota(jnp.int32, (1, LANES), 1)
        lanem = lane < HEAD
        qk_gain = prow(bqg_ref)[:, 0:LANES] * prow(bkg_ref) * (HEAD ** -0.5 * LOG2E)
        hs_bits = pltpu.bitcast(shared["hs"][TB - SUBLANES:TB, LANES:2 * LANES], jnp.int32)
        zero = (hs_bits & jnp.minimum(s_idx, 0))[0:1, :]
        qk_gain = pltpu.bitcast(pltpu.bitcast(qk_gain, jnp.int32) | zero, F32)
        lo = jnp.where(lanem, qk_gain, 0.0)
        hi = jnp.where(lanem, 0.0, qk_gain)
        first_tile_mask = jnp.where(seq_start, -jnp.inf, 0.0)

        def softmax_rows(sh, h, bi):
            if bi == 0:
                sh = jnp.concatenate([sh[:, 0:QBLK] + first_tile_mask, sh[:, QBLK:]], axis=1)
            sink = sinks_ref[layer * 4 + h] * LOG2E
            m = jnp.maximum(jnp.max(sh, axis=-1, keepdims=True), sink)
            p = jnp.exp2(sh - m)
            den = jnp.sum(p, axis=-1, keepdims=True) + jnp.exp2(sink - m)
            return p.astype(BF16), den

        for bi in range(TB // QBLK):
            r0 = bi * QBLK
            qa = qn[r0:r0 + QBLK, 0:LANES]
            qb = qn[r0:r0 + QBLK, LANES:2 * LANES]
            qst = jnp.concatenate([qa * lo, pltpu.roll(qa, HEAD, axis=1) * lo,
                                   pltpu.roll(qb, HEAD, axis=1) * hi, qb * hi], axis=0).astype(BF16)
            s = _dot_nt(qst, kbuf[r0:r0 + 2 * QBLK, :]) + biasm[...]
            yield
            pden = [softmax_rows(s[h * QBLK:(h + 1) * QBLK, :], h, bi) for h in range(4)]
            ov = _dot(jnp.concatenate([t[0] for t in pden], axis=0), vbuf[r0:r0 + 2 * QBLK, :])
            yield
            oh = [ov[h * QBLK:(h + 1) * QBLK, :] / pden[h][1] for h in range(4)]
            y01 = jnp.where(lanem, oh[0], pltpu.roll(oh[1], HEAD, axis=1))
            y23 = jnp.where(lanem, pltpu.roll(oh[2], HEAD, axis=1), oh[3])
            ycat[r0:r0 + QBLK, W:2 * W] = (jnp.concatenate([y01, y23], axis=1)
                                           * gates[r0:r0 + QBLK, W:2 * W]).astype(BF16)
        kbuf[0:QBLK, :] = kbuf[TB:TB + QBLK, :]
        vbuf[0:QBLK, :] = vbuf[TB:TB + QBLK, :]

    def mixer_c():
        xc = jnp.broadcast_to(prow(ccb_ref), (TB, W))
        xc = _causal_taps(xc, cbuf[...], ccw_ref, CONV_C, C_PAD)
        cbuf[0:C_PAD, :] = cbuf[TB:TB + C_PAD, :]
        xcb = xc.astype(BF16)
        rg = _sigmoid(_dot(xcb, cwr_ref[...]) + prow(cbr_ref))
        ig = _sigmoid(_dot(xcb, cwi_ref[...]) + prow(cbi_ref))
        yield
        nlam = -prow(lam_ref)
        softplus = jnp.maximum(nlam, 0.0) + jnp.log1p(jnp.exp(-jnp.abs(nlam)))
        a = jnp.exp2(rg * (softplus * (-LRU_C * LOG2E)))
        gap = 1.0 - a * a
        root = jnp.where(gap > 0.0, gap * lax.rsqrt(gap), 0.0)
        uu = root * (ig * xc)
        step = 1
        while step < TB:
            uu = a * _shift_rows(uu, step, 0.0) + uu
            a = a * _shift_rows(a, step, 1.0)
            step *= 2
        hs = uu + a * hcar[0:1, :]
        hcar[0:1, :] = hs[TB - 1:TB, :]
        shared["hs"] = hs
        ycat[:, 2 * W:3 * W] = (hs * gates[:, 2 * W:3 * W]).astype(BF16)

    def mixer_d():
        dq = pd[:, 0:LANES] * (D_KEY ** -0.5)
        dk = pd[:, LANES:2 * LANES]
        dv = pd[:, 2 * LANES:2 * LANES + W]
        z = _dot(plr[...].astype(BF16), dwup[...]) + prow(dbu_ref)
        yield
        lg = (jnp.minimum(z, 0.0) - jnp.log1p(jnp.exp2(jnp.abs(z) * -LOG2E))) * (LOG2E / GATE_TAU)
        bcum = None
        r = lg
        for p_ in range(2):
            part = r.astype(BF16)
            d_ = _dot(tri[...], part)
            bcum = d_ if bcum is None else bcum + d_
            r = r - part.astype(F32)
        yield
        lane = lax.broadcasted_iota(jnp.int32, (1, LANES), 1)
        lane256 = lax.broadcasted_iota(jnp.int32, (1, W), 1)
        ci = lax.broadcasted_iota(jnp.int32, (CHUNK, W), 0)
        cj = lax.broadcasted_iota(jnp.int32, (CHUNK, W), 1) % CHUNK
        causal = (ci >= cj).astype(F32)
        bdm = (lax.broadcasted_iota(jnp.int32, (W, LANES), 0) // HEAD
               == lax.broadcasted_iota(jnp.int32, (W, LANES), 1) // D_KEY).astype(F32)
        state = st[...]
        od_parts = []
        for c in range(TB // CHUNK):
            r0 = c * CHUNK
            bc = bcum[r0:r0 + CHUNK, :]
            bl = bc[CHUNK - 1:CHUNK, :]
            qt = (dq[r0:r0 + CHUNK, :] * jnp.exp2(bc)).astype(BF16)
            kc = dk[r0:r0 + CHUNK, :]
            kt = kc * jnp.exp2(-bc)
            ke = (kc * jnp.exp2(bl - bc)).astype(BF16)
            vc = dv[r0:r0 + CHUNK, :]
            kst = jnp.concatenate([kt * (lane // D_KEY == h).astype(F32) for h in range(D_HEADS)],
                                  axis=0).astype(BF16)
            att = (_dot_nt(qt, kst) * causal).astype(BF16)
            yield
            vbd = jnp.concatenate([vc * (lane256 // HEAD == h).astype(F32) for h in range(D_HEADS)],
                                  axis=0).astype(BF16)
            o_c = _dot(att, vbd) + _dot_nt(qt, state.astype(BF16))
            state = state * jnp.exp2(bl) + _dot_tn(vc.astype(BF16), ke) * bdm
            od_parts.append(o_c)
            yield
        st[...] = state
        od = jnp.concatenate(od_parts, axis=0)
        odn = od * lax.rsqrt(_dot_split(od * od, gmat, 1) + EPS) * prow(dng_ref)
        ycat[:, 3 * W:4 * W] = (odn * gates[:, 3 * W:4 * W]).astype(BF16)

    def out_proj_prev():
        for c0 in range(0, D_MODEL, OUT_SLAB):
            o_ref[0, :, c0:c0 + OUT_SLAB] = (xo_ref[0, :, c0:c0 + OUT_SLAB]
                                             + _dot(yprev[...], wout_ref[:, c0:c0 + OUT_SLAB]))
            yield

    @pl.when(s_idx < n_tiles)
    def main_step():
        nonlocal hb, gmat
        x = x_ref[0]
        ms = jnp.mean(x * x, axis=-1, keepdims=True)
        hb = (x * lax.rsqrt(ms + EPS) * prow(ng_ref)).astype(BF16)
        gmat = g256[...]
        gens = {"A": mixer_a(), "B": mixer_b(), "C": mixer_c(), "D": mixer_d(), "O": out_proj_prev(),
                "P": project()}
        for name in TRACE_ORDER:
            next(gens[name], None)
        running = [gens[k] for k in "PDABCO"]
        while running:
            running = [g for g in running if next(g, StopIteration) is not StopIteration]
        yprev[...] = ycat[...]

    @pl.when(s_idx == n_tiles)
    def last_step():
        for _ in out_proj_prev():
            pass


def _t5_bucket(dist):
    max_exact = N_BUCKETS // 2
    d = jnp.maximum(dist, 1).astype(F32)
    large = max_exact + (jnp.log(d / max_exact) / math.log(MAX_DISTANCE / max_exact)
                         * (N_BUCKETS - max_exact)).astype(jnp.int32)
    large = jnp.minimum(large, N_BUCKETS - 1)
    return jnp.where(dist < max_exact, dist, large)


def _layer_spec(layer, shape):
    return pl.BlockSpec((None,) + shape, lambda s: (layer,) + (0,) * len(shape))


def _whole(shape):
    return pl.BlockSpec(shape, lambda s: (0,) * len(shape))


def _layer_call(layer, tiles_per_seq, x, ng, win, wtail, wout, apw, cwr, cwi, dwup, aconv, vec256, ccw, dbu,
                dng, bqg, bkg, bucket, sinks, relb):
    n_tiles = x.shape[0]
    smem = pl.BlockSpec(memory_space=pltpu.SMEM)
    spec = functools.partial(_layer_spec, layer)
    acb, alg, alb, ccb, cbr, cbi, lam = vec256
    prev_tile = lambda s: (jnp.maximum(s - 1, 0), 0, 0)
    return pl.pallas_call(
        functools.partial(_layer_kernel, layer, tiles_per_seq, n_tiles),
        grid=(n_tiles + 1,),
        in_specs=[
            pl.BlockSpec((1, TB, D_MODEL), lambda s: (jnp.minimum(s, n_tiles - 1), 0, 0)),
            pl.BlockSpec((1, TB, D_MODEL), prev_tile),
            _whole((DEPTH, D_MODEL)), spec((D_MODEL, IN_COLS)), spec((D_MODEL, D_TAIL)),
            spec((D_MODEL, D_MODEL)), spec((W, W)), spec((W, W)), spec((W, W)),
            spec((GATE_RANK, LANES)), spec((CONV_A, W)),
            _whole((DEPTH, W)), _whole((DEPTH, W)), _whole((DEPTH, W)),
            spec((CONV_C, W)),
            _whole((DEPTH, W)), _whole((DEPTH, W)), _whole((DEPTH, W)), _whole((DEPTH, W)),
            _whole((DEPTH, LANES)),
            _whole((DEPTH, W)), _whole((DEPTH, W)), _whole((DEPTH, LANES)),
            _whole((QBLK, 2 * QBLK)),
            smem, smem,
        ],
        out_specs=pl.BlockSpec((1, TB, D_MODEL), prev_tile),
        out_shape=jax.ShapeDtypeStruct(x.shape, F32),
        scratch_shapes=[
            pltpu.VMEM((TB + A_PAD, W), F32),
            pltpu.VMEM((TB + C_PAD, W), F32),
            pltpu.VMEM((SUBLANES, W), F32),
            pltpu.VMEM((TB + QBLK, LANES), BF16),
            pltpu.VMEM((TB + QBLK, LANES), BF16),
            pltpu.VMEM((W, LANES), F32),
            pltpu.VMEM((TB, D_MODEL), BF16),
            pltpu.VMEM((4 * QBLK, 2 * QBLK), F32),
            pltpu.VMEM((W, W), BF16),
            pltpu.VMEM((TB, TB), BF16),
            pltpu.VMEM((LANES, LANES), BF16),
            pltpu.VMEM((TB, D_MODEL), F32),
            pltpu.VMEM((TB, B_QKV), F32),
            pltpu.VMEM((TB, D_QKV), F32),
            pltpu.VMEM((TB, LANES), F32),
            pltpu.VMEM((TB, D_MODEL), BF16),
        ],
        compiler_params=pltpu.CompilerParams(
            dimension_semantics=("arbitrary",),
            vmem_limit_bytes=VMEM_LIMIT_BYTES,
        ),
        name="hybrid_layer",
    )(x, x, ng, win, wtail, wout, apw, cwr, cwi, dwup, aconv, acb, alg, alb, ccw, ccb, cbr, cbi, lam, dbu,
      dng, bqg, bkg, bucket, sinks, relb)


def _block_diag(blocks):
    depth, n, c, _ = blocks.shape
    eye = jnp.eye(n, dtype=blocks.dtype)
    return (eye[None, :, None, :, None] * blocks[:, :, :, None, :]).reshape(depth, n * c, n * c)


def kernel(x, norm_g, w_in, a_conv_w, a_conv_b, a_ln_g, a_ln_b, a_pw, b_q_g, b_k_g, b_sinks, rel_bias,
           c_conv_w, c_conv_b, c_w_r, c_b_r, c_w_i, c_b_i, c_lambda, d_w_up, d_b_up, d_norm_g, w_out):
    dist = jnp.arange(QBLK)[:, None] + QBLK - jnp.arange(2 * QBLK)[None, :]
    bucket = _t5_bucket(jnp.clip(dist, 0, None)).astype(jnp.int32)
    wtail = jnp.concatenate([
        w_in[:, :, D_TAIL_SRC + GATE_RANK:IN_COLS], w_in[:, :, D_TAIL_SRC:D_TAIL_SRC + GATE_RANK],
        jnp.zeros((DEPTH, D_MODEL, D_TAIL - (IN_COLS - D_TAIL_SRC)), F32)], axis=2).astype(BF16)
    vec256 = (a_conv_b, a_ln_g, a_ln_b, c_conv_b, c_b_r, c_b_i, c_lambda)
    args = (norm_g, w_in.astype(BF16), wtail, w_out.astype(BF16), a_pw.astype(BF16),
            _block_diag(c_w_r).astype(BF16), _block_diag(c_w_i).astype(BF16), d_w_up, a_conv_w,
            vec256, c_conv_w, d_b_up, jnp.tile(d_norm_g, (1, 4)), jnp.tile(b_q_g, (1, 4)),
            jnp.tile(b_k_g, (1, 2)), bucket, b_sinks.reshape(-1), rel_bias.reshape(-1))
    bsz, seq, _ = x.shape
    assert seq % TB == 0
    h = x.reshape(bsz * seq // TB, TB, D_MODEL)
    for l in range(DEPTH):
        h = _layer_call(l, seq // TB, h, *args)
    return h.reshape(bsz, seq, D_MODEL)
```

```python
import functools
import math

import jax
import jax.numpy as jnp
from jax import lax
from jax.experimental import pallas as pl
from jax.experimental.pallas import tpu as pltpu

F32 = jnp.float32
BF16 = jnp.bfloat16

D_MODEL = 1024
DEPTH = 4
W = 256
EPS = 1e-6
CONV_A = 31
A_PAD = 32
HEAD = 64
WINDOW = 128
QBLK = 128
N_BUCKETS = 32
MAX_DISTANCE = 128
CONV_C = 4
C_PAD = 8
LRU_C = 8.0
D_HEADS = 4
D_KEY = 32
GATE_RANK = 16
GATE_TAU = 16.0
CHUNK = 64
SUBLANES = 8
LANES = 128
VMEM_LIMIT_BYTES = 48 * 1024 * 1024
LOG2E = math.log2(math.e)

TB = 256
TRACE_ORDER = "O P A P P C P P B D P C O D B P D D D D D D D D B A O P B B P O A A D B".split()
OUT_SLAB = 256

A_OFF = 0
B_OFF = 768
B_QKV = 512
C_OFF = 1536
D_OFF, D_QKV = 2048, 512
D_TAIL_SRC = D_OFF + D_QKV
D_TAIL = 384
IN_COLS = 2832


def _dot(a, b):
    return jnp.dot(a, b, preferred_element_type=F32)


def _dot_nt(a, b):
    return lax.dot_general(a, b, (((1,), (1,)), ((), ())), preferred_element_type=F32)


def _dot_tn(a, b):
    return lax.dot_general(a, b, (((0,), (0,)), ((), ())), preferred_element_type=F32)


def _dot_split(x, w_bf, passes):
    acc = None
    r = x
    for p in range(passes):
        part = r.astype(BF16)
        d = _dot(part, w_bf)
        acc = d if acc is None else acc + d
        if p + 1 < passes:
            r = r - part.astype(F32)
    return acc


def _sigmoid(x):
    return 1.0 / (1.0 + jnp.exp2(x * -LOG2E))


def _silu(x):
    return x * _sigmoid(x)


def _causal_taps(acc, buf, w_ref, n_taps, pad):
    rows = buf.shape[0]
    for res in range(SUBLANES):
        shifted = None
        for j in range(n_taps):
            off = pad - (n_taps - 1) + j
            if off % SUBLANES != res:
                continue
            if shifted is None:
                shifted = buf if res == 0 else pltpu.roll(buf, rows - res, axis=0)
            base = off - res
            acc = acc + w_ref[j:j + 1, :] * shifted[base:base + TB]
    return acc


def _shift_rows(x, step, fill):
    if step % SUBLANES == 0:
        return jnp.concatenate([jnp.full((step, x.shape[1]), fill, F32), x[0:x.shape[0] - step]], axis=0)
    rolled = pltpu.roll(x, step, axis=0)
    row = lax.broadcasted_iota(jnp.int32, (SUBLANES, x.shape[1]), 0)
    head = jnp.where(row < step, fill, rolled[0:SUBLANES])
    return jnp.concatenate([head, rolled[SUBLANES:]], axis=0)


def _layer_kernel(layer, tiles_per_seq, n_tiles,
                  x_ref, xo_ref, ng_ref, win_ref, wtail_ref, wout_ref, apw_ref, cwr_ref, cwi_ref, dwup_ref, aconv_ref,
                  acb_ref, alg_ref, alb_ref, ccw_ref, ccb_ref, cbr_ref, cbi_ref, lam_ref, dbu_ref,
                  dng_ref, bqg_ref, bkg_ref, bucket_ref, sinks_ref, relb_ref,
                  o_ref,
                  ubuf, cbuf, hcar, kbuf, vbuf, st, ycat, biasm, g256, tri, dwup, gates, pb, pd, plr, yprev, winb):
    s_idx = pl.program_id(0)
    seq_start = (s_idx % tiles_per_seq) == 0
    prow = lambda ref: ref[layer:layer + 1, :]

    @pl.when(s_idx == 0)
    def _build_tables():
        yprev[...] = jnp.zeros((TB, D_MODEL), BF16)
        for c0 in range(0, D_TAIL_SRC, 2 * W):
            winb[:, c0:c0 + 2 * W] = win_ref[:, c0:c0 + 2 * W].astype(BF16)
        ri = lax.broadcasted_iota(jnp.int32, (W, W), 0) // HEAD
        ci = lax.broadcasted_iota(jnp.int32, (W, W), 1) // HEAD
        g256[...] = jnp.where(ri == ci, 1.0 / HEAD, 0.0).astype(BF16)
        rt = lax.broadcasted_iota(jnp.int32, (TB, TB), 0)
        ct = lax.broadcasted_iota(jnp.int32, (TB, TB), 1)
        tri[...] = jnp.where((rt // CHUNK == ct // CHUNK) & (rt >= ct), 1.0, 0.0).astype(BF16)
        dwup[...] = jnp.zeros((LANES, LANES), BF16)
        dwup[0:GATE_RANK, :] = dwup_ref[...].astype(BF16)
        bucket = bucket_ref[...]
        qi = lax.broadcasted_iota(jnp.int32, (QBLK, 2 * QBLK), 0)
        kj = lax.broadcasted_iota(jnp.int32, (QBLK, 2 * QBLK), 1)
        dist = qi + QBLK - kj
        valid = (dist >= 0) & (dist < WINDOW)
        for h in range(4):
            acc = jnp.zeros((QBLK, 2 * QBLK), F32)
            for bk in range(N_BUCKETS):
                acc = jnp.where(bucket == bk, relb_ref[bk * 4 + h], acc)
            biasm[h * QBLK:(h + 1) * QBLK, :] = jnp.where(valid, acc * LOG2E, -jnp.inf)

    @pl.when(seq_start)
    def _reset_state():
        ubuf[0:A_PAD, :] = jnp.zeros((A_PAD, W), F32)
        cbuf[0:C_PAD, :] = jnp.zeros((C_PAD, W), F32)
        hcar[...] = jnp.zeros((SUBLANES, W), F32)
        kbuf[0:QBLK, :] = jnp.zeros((QBLK, LANES), BF16)
        vbuf[0:QBLK, :] = jnp.zeros((QBLK, LANES), BF16)
        st[...] = jnp.zeros((W, LANES), F32)

    hb = gmat = None

    proj = lambda off, width: _dot(hb, winb[:, off:off + width])
    shared = {}

    def project():
        pav = proj(A_OFF, 2 * W)
        ubuf[A_PAD:A_PAD + TB, :] = pav[:, 0:W] * _sigmoid(pav[:, W:2 * W])
        yield
        cbuf[C_PAD:C_PAD + TB, :] = proj(C_OFF, W)
        yield
        pb[...] = proj(B_OFF, B_QKV)
        yield
        pd[...] = proj(D_OFF, D_QKV)
        yield
        plr[...] = _dot(hb, wtail_ref[:, W:W + LANES])
        yield
        gates[:, 2 * W:3 * W] = _silu(proj(C_OFF + W, W))
        yield
        gates[:, 0:W] = _silu(proj(A_OFF + 2 * W, W))
        yield
        gates[:, W:2 * W] = _silu(proj(B_OFF + B_QKV, W))
        yield
        gates[:, 3 * W:4 * W] = _silu(_dot(hb, wtail_ref[:, 0:W]))


    def mixer_a():
        conv = jnp.broadcast_to(prow(acb_ref), (TB, W))
        conv = _causal_taps(conv, ubuf[...], aconv_ref, CONV_A, A_PAD)
        ubuf[0:A_PAD, :] = ubuf[TB:TB + A_PAD, :]
        yield
        mu = _dot_split(conv, gmat, 2)
        yield
        dc = conv - mu
        var = _dot_split(dc * dc, gmat, 1)
        yield
        un = dc * lax.rsqrt(var + EPS) * prow(alg_ref) + prow(alb_ref)
        ya = _dot(_silu(un).astype(BF16), apw_ref[...]) * gates[:, 0:W]
        ycat[:, 0:W] = ya.astype(BF16)

    def mixer_b():
        q = pb[:, 0:W]
        k = pb[:, W:W + LANES]
        v = pb[:, W + LANES:W + 2 * LANES]
        qn = q * lax.rsqrt(_dot_split(q * q, gmat, 1) + EPS)
        kn = k * lax.rsqrt(_dot_split(k * k, gmat[0:LANES, 0:LANES], 1) + EPS)
        yield
        kbuf[QBLK:QBLK + TB, :] = kn.astype(BF16)
        vbuf[QBLK:QBLK + TB, :] = v.astype(BF16)
        lane = lax.broadcasted_iota(jnp.int32, (1, LANES), 1)
        lanem = lane < HEAD
        qk_gain = prow(bqg_ref)[:, 0:LANES] * prow(bkg_ref) * (HEAD ** -0.5 * LOG2E)
        hs_bits = pltpu.bitcast(shared["hs"][TB - SUBLANES:TB, LANES:2 * LANES], jnp.int32)
        zero = (hs_bits & jnp.minimum(s_idx, 0))[0:1, :]
        qk_gain = pltpu.bitcast(pltpu.bitcast(qk_gain, jnp.int32) | zero, F32)
        lo = jnp.where(lanem, qk_gain, 0.0)
        hi = jnp.where(lanem, 0.0, qk_gain)
        first_tile_mask = jnp.where(seq_start, -jnp.inf, 0.0)

        def softmax_rows(sh, h, bi):
            if bi == 0:
                sh = jnp.concatenate([sh[:, 0:QBLK] + first_tile_mask, sh[:, QBLK:]], axis=1)
            sink = sinks_ref[layer * 4 + h] * LOG2E
            m = jnp.maximum(jnp.max(sh, axis=-1, keepdims=True), sink)
            p = jnp.exp2(sh - m)
            den = jnp.sum(p, axis=-1, keepdims=True) + jnp.exp2(sink - m)
            return p.astype(BF16), den

        for bi in range(TB // QBLK):
            r0 = bi * QBLK
            qa = qn[r0:r0 + QBLK, 0:LANES]
            qb = qn[r0:r0 + QBLK, LANES:2 * LANES]
            qst = jnp.concatenate([qa * lo, pltpu.roll(qa, HEAD, axis=1) * lo,
                                   pltpu.roll(qb, HEAD, axis=1) * hi, qb * hi], axis=0).astype(BF16)
            s = _dot_nt(qst, kbuf[r0:r0 + 2 * QBLK, :]) + biasm[...]
            yield
            pden = [softmax_rows(s[h * QBLK:(h + 1) * QBLK, :], h, bi) for h in range(4)]
            ov = _dot(jnp.concatenate([t[0] for t in pden], axis=0), vbuf[r0:r0 + 2 * QBLK, :])
            yield
            oh = [ov[h * QBLK:(h + 1) * QBLK, :] / pden[h][1] for h in range(4)]
            y01 = jnp.where(lanem, oh[0], pltpu.roll(oh[1], HEAD, axis=1))
            y23 = jnp.where(lanem, pltpu.roll(oh[2], HEAD, axis=1), oh[3])
            ycat[r0:r0 + QBLK, W:2 * W] = (jnp.concatenate([y01, y23], axis=1)
                                           * gates[r0:r0 + QBLK, W:2 * W]).astype(BF16)
        kbuf[0:QBLK, :] = kbuf[TB:TB + QBLK, :]
        vbuf[0:QBLK, :] = vbuf[TB:TB + QBLK, :]

    def mixer_c():
        xc = jnp.broadcast_to(prow(ccb_ref), (TB, W))
        xc = _causal_taps(xc, cbuf[...], ccw_ref, CONV_C, C_PAD)
        cbuf[0:C_PAD, :] = cbuf[TB:TB + C_PAD, :]
        xcb = xc.astype(BF16)
        rg = _sigmoid(_dot(xcb, cwr_ref[...]) + prow(cbr_ref))
        ig = _sigmoid(_dot(xcb, cwi_ref[...]) + prow(cbi_ref))
        yield
        nlam = -prow(lam_ref)
        softplus = jnp.maximum(nlam, 0.0) + jnp.log1p(jnp.exp(-jnp.abs(nlam)))
        a = jnp.exp2(rg * (softplus * (-LRU_C * LOG2E)))
        gap = 1.0 - a * a
        root = jnp.where(gap > 0.0, gap * lax.rsqrt(gap), 0.0)
        uu = root * (ig * xc)
        step = 1
        while step < TB:
            uu = a * _shift_rows(uu, step, 0.0) + uu
            a = a * _shift_rows(a, step, 1.0)
            step *= 2
        hs = uu + a * hcar[0:1, :]
        hcar[0:1, :] = hs[TB - 1:TB, :]
        shared["hs"] = hs
        ycat[:, 2 * W:3 * W] = (hs * gates[:, 2 * W:3 * W]).astype(BF16)

    def mixer_d():
        dq = pd[:, 0:LANES] * (D_KEY ** -0.5)
        dk = pd[:, LANES:2 * LANES]
        dv = pd[:, 2 * LANES:2 * LANES + W]
        z = _dot(plr[...].astype(BF16), dwup[...]) + prow(dbu_ref)
        yield
        lg = (jnp.minimum(z, 0.0) - jnp.log1p(jnp.exp2(jnp.abs(z) * -LOG2E))) * (LOG2E / GATE_TAU)
        bcum = None
        r = lg
        for p_ in range(2):
            part = r.astype(BF16)
            d_ = _dot(tri[...], part)
            bcum = d_ if bcum is None else bcum + d_
            r = r - part.astype(F32)
        yield
        lane = lax.broadcasted_iota(jnp.int32, (1, LANES), 1)
        lane256 = lax.broadcasted_iota(jnp.int32, (1, W), 1)
        ci = lax.broadcasted_iota(jnp.int32, (CHUNK, W), 0)
        cj = lax.broadcasted_iota(jnp.int32, (CHUNK, W), 1) % CHUNK
        causal = (ci >= cj).astype(F32)
        bdm = (lax.broadcasted_iota(jnp.int32, (W, LANES), 0) // HEAD
               == lax.broadcasted_iota(jnp.int32, (W, LANES), 1) // D_KEY).astype(F32)
        state = st[...]
        od_parts = []
        for c in range(TB // CHUNK):
            r0 = c * CHUNK
            bc = bcum[r0:r0 + CHUNK, :]
            bl = bc[CHUNK - 1:CHUNK, :]
            qt = (dq[r0:r0 + CHUNK, :] * jnp.exp2(bc)).astype(BF16)
            kc = dk[r0:r0 + CHUNK, :]
            kt = kc * jnp.exp2(-bc)
            ke = (kc * jnp.exp2(bl - bc)).astype(BF16)
            vc = dv[r0:r0 + CHUNK, :]
            kst = jnp.concatenate([kt * (lane // D_KEY == h).astype(F32) for h in range(D_HEADS)],
                                  axis=0).astype(BF16)
            att = (_dot_nt(qt, kst) * causal).astype(BF16)
            yield
            vbd = jnp.concatenate([vc * (lane256 // HEAD == h).astype(F32) for h in range(D_HEADS)],
                                  axis=0).astype(BF16)
            o_c = _dot(att, vbd) + _dot_nt(qt, state.astype(BF16))
            state = state * jnp.exp2(bl) + _dot_tn(vc.astype(BF16), ke) * bdm
            od_parts.append(o_c)
            yield
        st[...] = state
        od = jnp.concatenate(od_parts, axis=0)
        odn = od * lax.rsqrt(_dot_split(od * od, gmat, 1) + EPS) * prow(dng_ref)
        ycat[:, 3 * W:4 * W] = (odn * gates[:, 3 * W:4 * W]).astype(BF16)

    def out_proj_prev():
        for c0 in range(0, D_MODEL, OUT_SLAB):
            o_ref[0, :, c0:c0 + OUT_SLAB] = (xo_ref[0, :, c0:c0 + OUT_SLAB]
                                             + _dot(yprev[...], wout_ref[:, c0:c0 + OUT_SLAB]))
            yield

    @pl.when(s_idx < n_tiles)
    def main_step():
        nonlocal hb, gmat
        x = x_ref[0]
        ms = jnp.mean(x * x, axis=-1, keepdims=True)
        hb = (x * lax.rsqrt(ms + EPS) * prow(ng_ref)).astype(BF16)
        gmat = g256[...]
        gens = {"A": mixer_a(), "B": mixer_b(), "C": mixer_c(), "D": mixer_d(), "O": out_proj_prev(),
                "P": project()}
        for name in TRACE_ORDER:
            next(gens[name], None)
        running = [gens[k] for k in "PDABCO"]
        while running:
            running = [g for g in running if next(g, StopIteration) is not StopIteration]
        yprev[...] = ycat[...]

    @pl.when(s_idx == n_tiles)
    def last_step():
        for _ in out_proj_prev():
            pass


def _t5_bucket(dist):
    max_exact = N_BUCKETS // 2
    d = jnp.maximum(dist, 1).astype(F32)
    large = max_exact + (jnp.log(d / max_exact) / math.log(MAX_DISTANCE / max_exact)
                         * (N_BUCKETS - max_exact)).astype(jnp.int32)
    large = jnp.minimum(large, N_BUCKETS - 1)
    return jnp.where(dist < max_exact, dist, large)


def _layer_spec(layer, shape):
    return pl.BlockSpec((None,) + shape, lambda s: (layer,) + (0,) * len(shape))


def _whole(shape):
    return pl.BlockSpec(shape, lambda s: (0,) * len(shape))


def _layer_call(layer, tiles_per_seq, x, ng, win, wtail, wout, apw, cwr, cwi, dwup, aconv, vec256, ccw, dbu,
                dng, bqg, bkg, bucket, sinks, relb):
    n_tiles = x.shape[0]
    smem = pl.BlockSpec(memory_space=pltpu.SMEM)
    spec = functools.partial(_layer_spec, layer)
    acb, alg, alb, ccb, cbr, cbi, lam = vec256
    prev_tile = lambda s: (jnp.maximum(s - 1, 0), 0, 0)
    return pl.pallas_call(
        functools.partial(_layer_kernel, layer, tiles_per_seq, n_tiles),
        grid=(n_tiles + 1,),
        in_specs=[
            pl.BlockSpec((1, TB, D_MODEL), lambda s: (jnp.minimum(s, n_tiles - 1), 0, 0)),
            pl.BlockSpec((1, TB, D_MODEL), prev_tile),
            _whole((DEPTH, D_MODEL)),
            pl.BlockSpec((None, D_MODEL, IN_COLS), lambda s: (layer, 0, 0), pipeline_mode=pl.Buffered(1)),
            spec((D_MODEL, D_TAIL)),
            spec((D_MODEL, D_MODEL)), spec((W, W)), spec((W, W)), spec((W, W)),
            spec((GATE_RANK, LANES)), spec((CONV_A, W)),
            _whole((DEPTH, W)), _whole((DEPTH, W)), _whole((DEPTH, W)),
            spec((CONV_C, W)),
            _whole((DEPTH, W)), _whole((DEPTH, W)), _whole((DEPTH, W)), _whole((DEPTH, W)),
            _whole((DEPTH, LANES)),
            _whole((DEPTH, W)), _whole((DEPTH, W)), _whole((DEPTH, LANES)),
            _whole((QBLK, 2 * QBLK)),
            smem, smem,
        ],
        out_specs=pl.BlockSpec((1, TB, D_MODEL), prev_tile),
        out_shape=jax.ShapeDtypeStruct(x.shape, F32),
        scratch_shapes=[
            pltpu.VMEM((TB + A_PAD, W), F32),
            pltpu.VMEM((TB + C_PAD, W), F32),
            pltpu.VMEM((SUBLANES, W), F32),
            pltpu.VMEM((TB + QBLK, LANES), BF16),
            pltpu.VMEM((TB + QBLK, LANES), BF16),
            pltpu.VMEM((W, LANES), F32),
            pltpu.VMEM((TB, D_MODEL), BF16),
            pltpu.VMEM((4 * QBLK, 2 * QBLK), F32),
            pltpu.VMEM((W, W), BF16),
            pltpu.VMEM((TB, TB), BF16),
            pltpu.VMEM((LANES, LANES), BF16),
            pltpu.VMEM((TB, D_MODEL), F32),
            pltpu.VMEM((TB, B_QKV), F32),
            pltpu.VMEM((TB, D_QKV), F32),
            pltpu.VMEM((TB, LANES), F32),
            pltpu.VMEM((TB, D_MODEL), BF16),
            pltpu.VMEM((D_MODEL, D_TAIL_SRC), BF16),
        ],
        compiler_params=pltpu.CompilerParams(
            dimension_semantics=("arbitrary",),
            vmem_limit_bytes=VMEM_LIMIT_BYTES,
        ),
        name="hybrid_layer",
    )(x, x, ng, win, wtail, wout, apw, cwr, cwi, dwup, aconv, acb, alg, alb, ccw, ccb, cbr, cbi, lam, dbu,
      dng, bqg, bkg, bucket, sinks, relb)


def _block_diag(blocks):
    depth, n, c, _ = blocks.shape
    eye = jnp.eye(n, dtype=blocks.dtype)
    return (eye[None, :, None, :, None] * blocks[:, :, :, None, :]).reshape(depth, n * c, n * c)


def kernel(x, norm_g, w_in, a_conv_w, a_conv_b, a_ln_g, a_ln_b, a_pw, b_q_g, b_k_g, b_sinks, rel_bias,
           c_conv_w, c_conv_b, c_w_r, c_b_r, c_w_i, c_b_i, c_lambda, d_w_up, d_b_up, d_norm_g, w_out):
    dist = jnp.arange(QBLK)[:, None] + QBLK - jnp.arange(2 * QBLK)[None, :]
    bucket = _t5_bucket(jnp.clip(dist, 0, None)).astype(jnp.int32)
    wtail = jnp.concatenate([
        w_in[:, :, D_TAIL_SRC + GATE_RANK:IN_COLS], w_in[:, :, D_TAIL_SRC:D_TAIL_SRC + GATE_RANK],
        jnp.zeros((DEPTH, D_MODEL, D_TAIL - (IN_COLS - D_TAIL_SRC)), F32)], axis=2).astype(BF16)
    vec256 = (a_conv_b, a_ln_g, a_ln_b, c_conv_b, c_b_r, c_b_i, c_lambda)
    args = (norm_g, w_in, wtail, w_out.astype(BF16), a_pw.astype(BF16),
            _block_diag(c_w_r).astype(BF16), _block_diag(c_w_i).astype(BF16), d_w_up, a_conv_w,
            vec256, c_conv_w, d_b_up, jnp.tile(d_norm_g, (1, 4)), jnp.tile(b_q_g, (1, 4)),
            jnp.tile(b_k_g, (1, 2)), bucket, b_sinks.reshape(-1), rel_bias.reshape(-1))
    bsz, seq, _ = x.shape
    assert seq % TB == 0
    h = x.reshape(bsz * seq // TB, TB, D_MODEL)
    for l in range(DEPTH):
        h = _layer_call(l, seq // TB, h, *args)
    return h.reshape(bsz, seq, D_MODEL)
```

```python
import functools
import math

import jax
import jax.numpy as jnp
from jax import lax
from jax.experimental import pallas as pl
from jax.experimental.pallas import tpu as pltpu

F32 = jnp.float32
BF16 = jnp.bfloat16

D_MODEL = 1024
DEPTH = 4
W = 256
EPS = 1e-6
CONV_A = 31
A_PAD = 32
HEAD = 64
WINDOW = 128
QBLK = 128
N_BUCKETS = 32
MAX_DISTANCE = 128
CONV_C = 4
C_PAD = 8
LRU_C = 8.0
D_HEADS = 4
D_KEY = 32
GATE_RANK = 16
GATE_TAU = 16.0
CHUNK = 64
SUBLANES = 8
LANES = 128
VMEM_LIMIT_BYTES = 48 * 1024 * 1024
LOG2E = math.log2(math.e)

TB = 512
TRACE_ORDER = ("O P A P P C P P B D P C O D B P D D D D B P D D D D B D D D D B D D D D B A O B B B P O A A "
               "D B").split()
OUT_SLAB = 256

A_OFF = 0
B_OFF = 768
B_QKV = 512
C_OFF = 1536
D_OFF, D_QKV = 2048, 512
D_TAIL_SRC = D_OFF + D_QKV
D_TAIL = 384
IN_COLS = 2832


def _dot(a, b):
    return jnp.dot(a, b, preferred_element_type=F32)


def _dot_nt(a, b):
    return lax.dot_general(a, b, (((1,), (1,)), ((), ())), preferred_element_type=F32)


def _dot_tn(a, b):
    return lax.dot_general(a, b, (((0,), (0,)), ((), ())), preferred_element_type=F32)


def _dot_split(x, w_bf, passes):
    acc = None
    r = x
    for p in range(passes):
        part = r.astype(BF16)
        d = _dot(part, w_bf)
        acc = d if acc is None else acc + d
        if p + 1 < passes:
            r = r - part.astype(F32)
    return acc


def _sigmoid(x):
    return 1.0 / (1.0 + jnp.exp2(x * -LOG2E))


def _silu(x):
    return x * _sigmoid(x)


def _causal_taps(acc, buf, w_ref, n_taps, pad):
    rows = buf.shape[0]
    for res in range(SUBLANES):
        shifted = None
        for j in range(n_taps):
            off = pad - (n_taps - 1) + j
            if off % SUBLANES != res:
                continue
            if shifted is None:
                shifted = buf if res == 0 else pltpu.roll(buf, rows - res, axis=0)
            base = off - res
            acc = acc + w_ref[j:j + 1, :] * shifted[base:base + TB]
    return acc


def _shift_rows(x, step, fill):
    if step % SUBLANES == 0:
        return jnp.concatenate([jnp.full((step, x.shape[1]), fill, F32), x[0:x.shape[0] - step]], axis=0)
    rolled = pltpu.roll(x, step, axis=0)
    row = lax.broadcasted_iota(jnp.int32, (SUBLANES, x.shape[1]), 0)
    head = jnp.where(row < step, fill, rolled[0:SUBLANES])
    return jnp.concatenate([head, rolled[SUBLANES:]], axis=0)


def _layer_kernel(layer, tiles_per_seq, n_tiles,
                  x_ref, xo_ref, ng_ref, win_ref, wtail_ref, wout_ref, apw_ref, cwr_ref, cwi_ref, dwup_ref, aconv_ref,
                  acb_ref, alg_ref, alb_ref, ccw_ref, ccb_ref, cbr_ref, cbi_ref, lam_ref, dbu_ref,
                  dng_ref, bqg_ref, bkg_ref, bucket_ref, sinks_ref, relb_ref,
                  o_ref,
                  ubuf, cbuf, hcar, kbuf, vbuf, st, ycat, biasm, g256, tri, dwup, gates, pb, pd, plr, yprev):
    s_idx = pl.program_id(0)
    seq_start = (s_idx % tiles_per_seq) == 0
    prow = lambda ref: ref[layer:layer + 1, :]

    @pl.when(s_idx == 0)
    def _build_tables():
        yprev[...] = jnp.zeros((TB, D_MODEL), BF16)
        ri = lax.broadcasted_iota(jnp.int32, (W, W), 0) // HEAD
        ci = lax.broadcasted_iota(jnp.int32, (W, W), 1) // HEAD
        g256[...] = jnp.where(ri == ci, 1.0 / HEAD, 0.0).astype(BF16)
        rt = lax.broadcasted_iota(jnp.int32, (TB, TB), 0)
        ct = lax.broadcasted_iota(jnp.int32, (TB, TB), 1)
        tri[...] = jnp.where((rt // CHUNK == ct // CHUNK) & (rt >= ct), 1.0, 0.0).astype(BF16)
        dwup[...] = jnp.zeros((LANES, LANES), BF16)
        dwup[0:GATE_RANK, :] = dwup_ref[...].astype(BF16)
        bucket = bucket_ref[...]
        qi = lax.broadcasted_iota(jnp.int32, (QBLK, 2 * QBLK), 0)
        kj = lax.broadcasted_iota(jnp.int32, (QBLK, 2 * QBLK), 1)
        dist = qi + QBLK - kj
        valid = (dist >= 0) & (dist < WINDOW)
        for h in range(4):
            acc = jnp.zeros((QBLK, 2 * QBLK), F32)
            for bk in range(N_BUCKETS):
                acc = jnp.where(bucket == bk, relb_ref[bk * 4 + h], acc)
            biasm[h * QBLK:(h + 1) * QBLK, :] = jnp.where(valid, acc * LOG2E, -jnp.inf)

    @pl.when(seq_start)
    def _reset_state():
        ubuf[0:A_PAD, :] = jnp.zeros((A_PAD, W), F32)
        cbuf[0:C_PAD, :] = jnp.zeros((C_PAD, W), F32)
        hcar[...] = jnp.zeros((SUBLANES, W), F32)
        kbuf[0:QBLK, :] = jnp.zeros((QBLK, LANES), BF16)
        vbuf[0:QBLK, :] = jnp.zeros((QBLK, LANES), BF16)
        st[...] = jnp.zeros((W, LANES), F32)

    hb = gmat = None

    proj = lambda off, width: _dot(hb, win_ref[:, off:off + width])
    shared = {}

    def project():
        pav = proj(A_OFF, 2 * W)
        ubuf[A_PAD:A_PAD + TB, :] = pav[:, 0:W] * _sigmoid(pav[:, W:2 * W])
        yield
        cbuf[C_PAD:C_PAD + TB, :] = proj(C_OFF, W)
        yield
        pb[...] = proj(B_OFF, B_QKV)
        yield
        pd[...] = proj(D_OFF, D_QKV)
        yield
        plr[...] = _dot(hb, wtail_ref[:, W:W + LANES])
        yield
        gates[:, 2 * W:3 * W] = _silu(proj(C_OFF + W, W))
        yield
        gates[:, 0:W] = _silu(proj(A_OFF + 2 * W, W))
        yield
        gates[:, W:2 * W] = _silu(proj(B_OFF + B_QKV, W))
        yield
        gates[:, 3 * W:4 * W] = _silu(_dot(hb, wtail_ref[:, 0:W]))


    def mixer_a():
        conv = jnp.broadcast_to(prow(acb_ref), (TB, W))
        conv = _causal_taps(conv, ubuf[...], aconv_ref, CONV_A, A_PAD)
        ubuf[0:A_PAD, :] = ubuf[TB:TB + A_PAD, :]
        yield
        mu = _dot_split(conv, gmat, 2)
        yield
        dc = conv - mu
        var = _dot_split(dc * dc, gmat, 1)
        yield
        un = dc * lax.rsqrt(var + EPS) * prow(alg_ref) + prow(alb_ref)
        ya = _dot(_silu(un).astype(BF16), apw_ref[...]) * gates[:, 0:W]
        ycat[:, 0:W] = ya.astype(BF16)

    def mixer_b():
        q = pb[:, 0:W]
        k = pb[:, W:W + LANES]
        v = pb[:, W + LANES:W + 2 * LANES]
        qn = q * lax.rsqrt(_dot_split(q * q, gmat, 1) + EPS)
        kn = k * lax.rsqrt(_dot_split(k * k, gmat[0:LANES, 0:LANES], 1) + EPS)
        yield
        kbuf[QBLK:QBLK + TB, :] = kn.astype(BF16)
        vbuf[QBLK:QBLK + TB, :] = v.astype(BF16)
        lane = lax.broadcasted_iota(jnp.int32, (1, LANES), 1)
        lanem = lane < HEAD
        qk_gain = prow(bqg_ref)[:, 0:LANES] * prow(bkg_ref) * (HEAD ** -0.5 * LOG2E)
        hs_bits = pltpu.bitcast(shared["hs"][TB - SUBLANES:TB, LANES:2 * LANES], jnp.int32)
        zero = (hs_bits & jnp.minimum(s_idx, 0))[0:1, :]
        qk_gain = pltpu.bitcast(pltpu.bitcast(qk_gain, jnp.int32) | zero, F32)
        lo = jnp.where(lanem, qk_gain, 0.0)
        hi = jnp.where(lanem, 0.0, qk_gain)
        first_tile_mask = jnp.where(seq_start, -jnp.inf, 0.0)

        def softmax_rows(sh, h, bi):
            if bi == 0:
                sh = jnp.concatenate([sh[:, 0:QBLK] + first_tile_mask, sh[:, QBLK:]], axis=1)
            sink = sinks_ref[layer * 4 + h] * LOG2E
            m = jnp.maximum(jnp.max(sh, axis=-1, keepdims=True), sink)
            p = jnp.exp2(sh - m)
            den = jnp.sum(p, axis=-1, keepdims=True) + jnp.exp2(sink - m)
            return p.astype(BF16), den

        for bi in range(TB // QBLK):
            r0 = bi * QBLK
            qa = qn[r0:r0 + QBLK, 0:LANES]
            qb = qn[r0:r0 + QBLK, LANES:2 * LANES]
            qst = jnp.concatenate([qa * lo, pltpu.roll(qa, HEAD, axis=1) * lo,
                                   pltpu.roll(qb, HEAD, axis=1) * hi, qb * hi], axis=0).astype(BF16)
            s = _dot_nt(qst, kbuf[r0:r0 + 2 * QBLK, :]) + biasm[...]
            yield
            pden = [softmax_rows(s[h * QBLK:(h + 1) * QBLK, :], h, bi) for h in range(4)]
            ov = _dot(jnp.concatenate([t[0] for t in pden], axis=0), vbuf[r0:r0 + 2 * QBLK, :])
            yield
            oh = [ov[h * QBLK:(h + 1) * QBLK, :] / pden[h][1] for h in range(4)]
            y01 = jnp.where(lanem, oh[0], pltpu.roll(oh[1], HEAD, axis=1))
            y23 = jnp.where(lanem, pltpu.roll(oh[2], HEAD, axis=1), oh[3])
            ycat[r0:r0 + QBLK, W:2 * W] = (jnp.concatenate([y01, y23], axis=1)
                                           * gates[r0:r0 + QBLK, W:2 * W]).astype(BF16)
        kbuf[0:QBLK, :] = kbuf[TB:TB + QBLK, :]
        vbuf[0:QBLK, :] = vbuf[TB:TB + QBLK, :]

    def mixer_c():
        xc = jnp.broadcast_to(prow(ccb_ref), (TB, W))
        xc = _causal_taps(xc, cbuf[...], ccw_ref, CONV_C, C_PAD)
        cbuf[0:C_PAD, :] = cbuf[TB:TB + C_PAD, :]
        xcb = xc.astype(BF16)
        rg = _sigmoid(_dot(xcb, cwr_ref[...]) + prow(cbr_ref))
        ig = _sigmoid(_dot(xcb, cwi_ref[...]) + prow(cbi_ref))
        yield
        nlam = -prow(lam_ref)
        softplus = jnp.maximum(nlam, 0.0) + jnp.log1p(jnp.exp(-jnp.abs(nlam)))
        a = jnp.exp2(rg * (softplus * (-LRU_C * LOG2E)))
        gap = 1.0 - a * a
        root = jnp.where(gap > 0.0, gap * lax.rsqrt(gap), 0.0)
        uu = root * (ig * xc)
        step = 1
        while step < TB:
            uu = a * _shift_rows(uu, step, 0.0) + uu
            a = a * _shift_rows(a, step, 1.0)
            step *= 2
        hs = uu + a * hcar[0:1, :]
        hcar[0:1, :] = hs[TB - 1:TB, :]
        shared["hs"] = hs
        ycat[:, 2 * W:3 * W] = (hs * gates[:, 2 * W:3 * W]).astype(BF16)

    def mixer_d():
        dq = pd[:, 0:LANES] * (D_KEY ** -0.5)
        dk = pd[:, LANES:2 * LANES]
        dv = pd[:, 2 * LANES:2 * LANES + W]
        z = _dot(plr[...].astype(BF16), dwup[...]) + prow(dbu_ref)
        yield
        lg = (jnp.minimum(z, 0.0) - jnp.log1p(jnp.exp2(jnp.abs(z) * -LOG2E))) * (LOG2E / GATE_TAU)
        bcum = None
        r = lg
        for p_ in range(2):
            part = r.astype(BF16)
            d_ = _dot(tri[...], part)
            bcum = d_ if bcum is None else bcum + d_
            r = r - part.astype(F32)
        yield
        lane = lax.broadcasted_iota(jnp.int32, (1, LANES), 1)
        lane256 = lax.broadcasted_iota(jnp.int32, (1, W), 1)
        ci = lax.broadcasted_iota(jnp.int32, (CHUNK, W), 0)
        cj = lax.broadcasted_iota(jnp.int32, (CHUNK, W), 1) % CHUNK
        causal = (ci >= cj).astype(F32)
        bdm = (lax.broadcasted_iota(jnp.int32, (W, LANES), 0) // HEAD
               == lax.broadcasted_iota(jnp.int32, (W, LANES), 1) // D_KEY).astype(F32)
        state = st[...]
        od_parts = []
        for c in range(TB // CHUNK):
            r0 = c * CHUNK
            bc = bcum[r0:r0 + CHUNK, :]
            bl = bc[CHUNK - 1:CHUNK, :]
            qt = (dq[r0:r0 + CHUNK, :] * jnp.exp2(bc)).astype(BF16)
            kc = dk[r0:r0 + CHUNK, :]
            kt = kc * jnp.exp2(-bc)
            ke = (kc * jnp.exp2(bl - bc)).astype(BF16)
            vc = dv[r0:r0 + CHUNK, :]
            kst = jnp.concatenate([kt * (lane // D_KEY == h).astype(F32) for h in range(D_HEADS)],
                                  axis=0).astype(BF16)
            att = (_dot_nt(qt, kst) * causal).astype(BF16)
            yield
            vbd = jnp.concatenate([vc * (lane256 // HEAD == h).astype(F32) for h in range(D_HEADS)],
                                  axis=0).astype(BF16)
            o_c = _dot(att, vbd) + _dot_nt(qt, state.astype(BF16))
            state = state * jnp.exp2(bl) + _dot_tn(vc.astype(BF16), ke) * bdm
            od_parts.append(o_c)
            yield
        st[...] = state
        od = jnp.concatenate(od_parts, axis=0)
        odn = od * lax.rsqrt(_dot_split(od * od, gmat, 1) + EPS) * prow(dng_ref)
        ycat[:, 3 * W:4 * W] = (odn * gates[:, 3 * W:4 * W]).astype(BF16)

    def out_proj_prev():
        for c0 in range(0, D_MODEL, OUT_SLAB):
            o_ref[0, :, c0:c0 + OUT_SLAB] = (xo_ref[0, :, c0:c0 + OUT_SLAB]
                                             + _dot(yprev[...], wout_ref[:, c0:c0 + OUT_SLAB]))
            yield

    @pl.when(s_idx < n_tiles)
    def main_step():
        nonlocal hb, gmat
        x = x_ref[0]
        ms = jnp.mean(x * x, axis=-1, keepdims=True)
        hb = (x * lax.rsqrt(ms + EPS) * prow(ng_ref)).astype(BF16)
        gmat = g256[...]
        gens = {"A": mixer_a(), "B": mixer_b(), "C": mixer_c(), "D": mixer_d(), "O": out_proj_prev(),
                "P": project()}
        for name in TRACE_ORDER:
            next(gens[name], None)
        running = [gens[k] for k in "PDABCO"]
        while running:
            running = [g for g in running if next(g, StopIteration) is not StopIteration]
        yprev[...] = ycat[...]

    @pl.when(s_idx == n_tiles)
    def last_step():
        for _ in out_proj_prev():
            pass


def _t5_bucket(dist):
    max_exact = N_BUCKETS // 2
    d = jnp.maximum(dist, 1).astype(F32)
    large = max_exact + (jnp.log(d / max_exact) / math.log(MAX_DISTANCE / max_exact)
                         * (N_BUCKETS - max_exact)).astype(jnp.int32)
    large = jnp.minimum(large, N_BUCKETS - 1)
    return jnp.where(dist < max_exact, dist, large)


def _layer_spec(layer, shape):
    return pl.BlockSpec((None,) + shape, lambda s: (layer,) + (0,) * len(shape))


def _whole(shape):
    return pl.BlockSpec(shape, lambda s: (0,) * len(shape))


def _layer_call(layer, tiles_per_seq, x, ng, win, wtail, wout, apw, cwr, cwi, dwup, aconv, vec256, ccw, dbu,
                dng, bqg, bkg, bucket, sinks, relb):
    n_tiles = x.shape[0]
    smem = pl.BlockSpec(memory_space=pltpu.SMEM)
    spec = functools.partial(_layer_spec, layer)
    acb, alg, alb, ccb, cbr, cbi, lam = vec256
    prev_tile = lambda s: (jnp.maximum(s - 1, 0), 0, 0)
    return pl.pallas_call(
        functools.partial(_layer_kernel, layer, tiles_per_seq, n_tiles),
        grid=(n_tiles + 1,),
        in_specs=[
            pl.BlockSpec((1, TB, D_MODEL), lambda s: (jnp.minimum(s, n_tiles - 1), 0, 0)),
            pl.BlockSpec((1, TB, D_MODEL), prev_tile),
            _whole((DEPTH, D_MODEL)), spec((D_MODEL, IN_COLS)), spec((D_MODEL, D_TAIL)),
            spec((D_MODEL, D_MODEL)), spec((W, W)), spec((W, W)), spec((W, W)),
            spec((GATE_RANK, LANES)), spec((CONV_A, W)),
            _whole((DEPTH, W)), _whole((DEPTH, W)), _whole((DEPTH, W)),
            spec((CONV_C, W)),
            _whole((DEPTH, W)), _whole((DEPTH, W)), _whole((DEPTH, W)), _whole((DEPTH, W)),
            _whole((DEPTH, LANES)),
            _whole((DEPTH, W)), _whole((DEPTH, W)), _whole((DEPTH, LANES)),
            _whole((QBLK, 2 * QBLK)),
            smem, smem,
        ],
        out_specs=pl.BlockSpec((1, TB, D_MODEL), prev_tile),
        out_shape=jax.ShapeDtypeStruct(x.shape, F32),
        scratch_shapes=[
            pltpu.VMEM((TB + A_PAD, W), F32),
            pltpu.VMEM((TB + C_PAD, W), F32),
            pltpu.VMEM((SUBLANES, W), F32),
            pltpu.VMEM((TB + QBLK, LANES), BF16),
            pltpu.VMEM((TB + QBLK, LANES), BF16),
            pltpu.VMEM((W, LANES), F32),
            pltpu.VMEM((TB, D_MODEL), BF16),
            pltpu.VMEM((4 * QBLK, 2 * QBLK), F32),
            pltpu.VMEM((W, W), BF16),
            pltpu.VMEM((TB, TB), BF16),
            pltpu.VMEM((LANES, LANES), BF16),
            pltpu.VMEM((TB, D_MODEL), F32),
            pltpu.VMEM((TB, B_QKV), F32),
            pltpu.VMEM((TB, D_QKV), F32),
            pltpu.VMEM((TB, LANES), F32),
            pltpu.VMEM((TB, D_MODEL), BF16),
        ],
        compiler_params=pltpu.CompilerParams(
            dimension_semantics=("arbitrary",),
            vmem_limit_bytes=VMEM_LIMIT_BYTES,
        ),
        name="hybrid_layer",
    )(x, x, ng, win, wtail, wout, apw, cwr, cwi, dwup, aconv, acb, alg, alb, ccw, ccb, cbr, cbi, lam, dbu,
      dng, bqg, bkg, bucket, sinks, relb)


def _block_diag(blocks):
    depth, n, c, _ = blocks.shape
    eye = jnp.eye(n, dtype=blocks.dtype)
    return (eye[None, :, None, :, None] * blocks[:, :, :, None, :]).reshape(depth, n * c, n * c)


def kernel(x, norm_g, w_in, a_conv_w, a_conv_b, a_ln_g, a_ln_b, a_pw, b_q_g, b_k_g, b_sinks, rel_bias,
           c_conv_w, c_conv_b, c_w_r, c_b_r, c_w_i, c_b_i, c_lambda, d_w_up, d_b_up, d_norm_g, w_out):
    dist = jnp.arange(QBLK)[:, None] + QBLK - jnp.arange(2 * QBLK)[None, :]
    bucket = _t5_bucket(jnp.clip(dist, 0, None)).astype(jnp.int32)
    wtail = jnp.concatenate([
        w_in[:, :, D_TAIL_SRC + GATE_RANK:IN_COLS], w_in[:, :, D_TAIL_SRC:D_TAIL_SRC + GATE_RANK],
        jnp.zeros((DEPTH, D_MODEL, D_TAIL - (IN_COLS - D_TAIL_SRC)), F32)], axis=2).astype(BF16)
    vec256 = (a_conv_b, a_ln_g, a_ln_b, c_conv_b, c_b_r, c_b_i, c_lambda)
    args = (norm_g, w_in.astype(BF16), wtail, w_out.astype(BF16), a_pw.astype(BF16),
            _block_diag(c_w_r).astype(BF16), _block_diag(c_w_i).astype(BF16), d_w_up, a_conv_w,
            vec256, c_conv_w, d_b_up, jnp.tile(d_norm_g, (1, 4)), jnp.tile(b_q_g, (1, 4)),
            jnp.tile(b_k_g, (1, 2)), bucket, b_sinks.reshape(-1), rel_bias.reshape(-1))
    bsz, seq, _ = x.shape
    assert seq % TB == 0
    h = x.reshape(bsz * seq // TB, TB, D_MODEL)
    for l in range(DEPTH):
        h = _layer_call(l, seq // TB, h, *args)
    return h.reshape(bsz, seq, D_MODEL)
```

```python
import functools
import math

import jax
import jax.numpy as jnp
from jax import lax
from jax.experimental import pallas as pl
from jax.experimental.pallas import tpu as pltpu

F32 = jnp.float32
BF16 = jnp.bfloat16

D_MODEL = 1024
DEPTH = 4
W = 256
EPS = 1e-6
CONV_A = 31
A_PAD = 32
HEAD = 64
WINDOW = 128
QBLK = 128
N_BUCKETS = 32
MAX_DISTANCE = 128
CONV_C = 4
C_PAD = 8
LRU_C = 8.0
D_HEADS = 4
D_KEY = 32
GATE_RANK = 16
GATE_TAU = 16.0
CHUNK = 64
SUBLANES = 8
LANES = 128
VMEM_LIMIT_BYTES = 48 * 1024 * 1024
LOG2E = math.log2(math.e)

TB = 512
TRACE_ORDER = ("O P A P P C P P O B D P C O D B P D D D D B P D D D D B D D D D B O D D D D B P A B B B A A "
               "D B").split()
OUT_SLAB = 256

A_OFF = 0
B_OFF = 768
B_QKV = 512
C_OFF = 1536
D_OFF, D_QKV = 2048, 512
D_TAIL_SRC = D_OFF + D_QKV
D_TAIL = 384
IN_COLS = 2832


def _dot(a, b):
    return jnp.dot(a, b, preferred_element_type=F32)


def _dot_nt(a, b):
    return lax.dot_general(a, b, (((1,), (1,)), ((), ())), preferred_element_type=F32)


def _dot_tn(a, b):
    return lax.dot_general(a, b, (((0,), (0,)), ((), ())), preferred_element_type=F32)


def _dot_split(x, w_bf, passes):
    acc = None
    r = x
    for p in range(passes):
        part = r.astype(BF16)
        d = _dot(part, w_bf)
        acc = d if acc is None else acc + d
        if p + 1 < passes:
            r = r - part.astype(F32)
    return acc


def _sigmoid(x):
    return 1.0 / (1.0 + jnp.exp2(x * -LOG2E))


def _silu(x):
    return x * _sigmoid(x)


def _causal_taps(acc, buf, w_ref, n_taps, pad):
    rows = buf.shape[0]
    for res in range(SUBLANES):
        shifted = None
        for j in range(n_taps):
            off = pad - (n_taps - 1) + j
            if off % SUBLANES != res:
                continue
            if shifted is None:
                shifted = buf if res == 0 else pltpu.roll(buf, rows - res, axis=0)
            base = off - res
            acc = acc + w_ref[j:j + 1, :] * shifted[base:base + TB]
    return acc


def _shift_rows(x, step, fill):
    if step % SUBLANES == 0:
        return jnp.concatenate([jnp.full((step, x.shape[1]), fill, F32), x[0:x.shape[0] - step]], axis=0)
    rolled = pltpu.roll(x, step, axis=0)
    row = lax.broadcasted_iota(jnp.int32, (SUBLANES, x.shape[1]), 0)
    head = jnp.where(row < step, fill, rolled[0:SUBLANES])
    return jnp.concatenate([head, rolled[SUBLANES:]], axis=0)


def _layer_kernel(layer, tiles_per_seq, n_tiles,
                  x_ref, xo_ref, ng_ref, win_ref, wtail_ref, wout_ref, apw_ref, cwr_ref, cwi_ref, dwup_ref, aconv_ref,
                  acb_ref, alg_ref, alb_ref, ccw_ref, ccb_ref, cbr_ref, cbi_ref, lam_ref, dbu_ref,
                  dng_ref, bqg_ref, bkg_ref, bucket_ref, sinks_ref, relb_ref,
                  o_ref,
                  ubuf, cbuf, hcar, kbuf, vbuf, st, ycat, biasm, g256, tri, dwup, gates, pb, pd, plr, yprev):
    s_idx = pl.program_id(0)
    seq_start = (s_idx % tiles_per_seq) == 0
    prow = lambda ref: ref[layer:layer + 1, :]

    @pl.when(s_idx == 0)
    def _build_tables():
        yprev[...] = jnp.zeros((TB, D_MODEL), BF16)
        ri = lax.broadcasted_iota(jnp.int32, (W, W), 0) // HEAD
        ci = lax.broadcasted_iota(jnp.int32, (W, W), 1) // HEAD
        g256[...] = jnp.where(ri == ci, 1.0 / HEAD, 0.0).astype(BF16)
        rt = lax.broadcasted_iota(jnp.int32, (TB, TB), 0)
        ct = lax.broadcasted_iota(jnp.int32, (TB, TB), 1)
        tri[...] = jnp.where((rt // CHUNK == ct // CHUNK) & (rt >= ct), 1.0, 0.0).astype(BF16)
        dwup[...] = jnp.zeros((LANES, LANES), BF16)
        dwup[0:GATE_RANK, :] = dwup_ref[...].astype(BF16)
        bucket = bucket_ref[...]
        qi = lax.broadcasted_iota(jnp.int32, (QBLK, 2 * QBLK), 0)
        kj = lax.broadcasted_iota(jnp.int32, (QBLK, 2 * QBLK), 1)
        dist = qi + QBLK - kj
        valid = (dist >= 0) & (dist < WINDOW)
        for h in range(4):
            acc = jnp.zeros((QBLK, 2 * QBLK), F32)
            for bk in range(N_BUCKETS):
                acc = jnp.where(bucket == bk, relb_ref[bk * 4 + h], acc)
            biasm[h * QBLK:(h + 1) * QBLK, :] = jnp.where(valid, acc * LOG2E, -jnp.inf)

    @pl.when(seq_start)
    def _reset_state():
        ubuf[0:A_PAD, :] = jnp.zeros((A_PAD, W), F32)
        cbuf[0:C_PAD, :] = jnp.zeros((C_PAD, W), F32)
        hcar[...] = jnp.zeros((SUBLANES, W), F32)
        kbuf[0:QBLK, :] = jnp.zeros((QBLK, LANES), BF16)
        vbuf[0:QBLK, :] = jnp.zeros((QBLK, LANES), BF16)
        st[...] = jnp.zeros((W, LANES), F32)

    hb = gmat = None

    proj = lambda off, width: _dot(hb, win_ref[:, off:off + width])
    shared = {}

    def project():
        pav = proj(A_OFF, 2 * W)
        ubuf[A_PAD:A_PAD + TB, :] = pav[:, 0:W] * _sigmoid(pav[:, W:2 * W])
        yield
        cbuf[C_PAD:C_PAD + TB, :] = proj(C_OFF, W)
        yield
        pb[...] = proj(B_OFF, B_QKV)
        yield
        pd[...] = proj(D_OFF, D_QKV)
        yield
        plr[...] = _dot(hb, wtail_ref[:, W:W + LANES])
        yield
        gates[:, 2 * W:3 * W] = _silu(proj(C_OFF + W, W))
        yield
        gates[:, 0:W] = _silu(proj(A_OFF + 2 * W, W))
        yield
        gates[:, W:2 * W] = _silu(proj(B_OFF + B_QKV, W))
        yield
        gates[:, 3 * W:4 * W] = _silu(_dot(hb, wtail_ref[:, 0:W]))


    def mixer_a():
        conv = jnp.broadcast_to(prow(acb_ref), (TB, W))
        conv = _causal_taps(conv, ubuf[...], aconv_ref, CONV_A, A_PAD)
        ubuf[0:A_PAD, :] = ubuf[TB:TB + A_PAD, :]
        yield
        mu = _dot_split(conv, gmat, 2)
        yield
        dc = conv - mu
        var = _dot_split(dc * dc, gmat, 1)
        yield
        un = dc * lax.rsqrt(var + EPS) * prow(alg_ref) + prow(alb_ref)
        ya = _dot(_silu(un).astype(BF16), apw_ref[...]) * gates[:, 0:W]
        ycat[:, 0:W] = ya.astype(BF16)

    def mixer_b():
        q = pb[:, 0:W]
        k = pb[:, W:W + LANES]
        v = pb[:, W + LANES:W + 2 * LANES]
        qn = q * lax.rsqrt(_dot_split(q * q, gmat, 1) + EPS)
        kn = k * lax.rsqrt(_dot_split(k * k, gmat[0:LANES, 0:LANES], 1) + EPS)
        yield
        kbuf[QBLK:QBLK + TB, :] = kn.astype(BF16)
        vbuf[QBLK:QBLK + TB, :] = v.astype(BF16)
        lane = lax.broadcasted_iota(jnp.int32, (1, LANES), 1)
        lanem = lane < HEAD
        qk_gain = prow(bqg_ref)[:, 0:LANES] * prow(bkg_ref) * (HEAD ** -0.5 * LOG2E)
        hs_bits = pltpu.bitcast(shared["hs"][TB - SUBLANES:TB, LANES:2 * LANES], jnp.int32)
        zero = (hs_bits & jnp.minimum(s_idx, 0))[0:1, :]
        qk_gain = pltpu.bitcast(pltpu.bitcast(qk_gain, jnp.int32) | zero, F32)
        lo = jnp.where(lanem, qk_gain, 0.0)
        hi = jnp.where(lanem, 0.0, qk_gain)
        first_tile_mask = jnp.where(seq_start, -jnp.inf, 0.0)

        def softmax_rows(sh, h, bi):
            if bi == 0:
                sh = jnp.concatenate([sh[:, 0:QBLK] + first_tile_mask, sh[:, QBLK:]], axis=1)
            sink = sinks_ref[layer * 4 + h] * LOG2E
            m = jnp.maximum(jnp.max(sh, axis=-1, keepdims=True), sink)
            p = jnp.exp2(sh - m)
            den = jnp.sum(p, axis=-1, keepdims=True) + jnp.exp2(sink - m)
            return p.astype(BF16), den

        for bi in range(TB // QBLK):
            r0 = bi * QBLK
            qa = qn[r0:r0 + QBLK, 0:LANES]
            qb = qn[r0:r0 + QBLK, LANES:2 * LANES]
            qst = jnp.concatenate([qa * lo, pltpu.roll(qa, HEAD, axis=1) * lo,
                                   pltpu.roll(qb, HEAD, axis=1) * hi, qb * hi], axis=0).astype(BF16)
            s = _dot_nt(qst, kbuf[r0:r0 + 2 * QBLK, :]) + biasm[...]
            yield
            pden = [softmax_rows(s[h * QBLK:(h + 1) * QBLK, :], h, bi) for h in range(4)]
            ov = _dot(jnp.concatenate([t[0] for t in pden], axis=0), vbuf[r0:r0 + 2 * QBLK, :])
            yield
            oh = [ov[h * QBLK:(h + 1) * QBLK, :] / pden[h][1] for h in range(4)]
            y01 = jnp.where(lanem, oh[0], pltpu.roll(oh[1], HEAD, axis=1))
            y23 = jnp.where(lanem, pltpu.roll(oh[2], HEAD, axis=1), oh[3])
            ycat[r0:r0 + QBLK, W:2 * W] = (jnp.concatenate([y01, y23], axis=1)
                                           * gates[r0:r0 + QBLK, W:2 * W]).astype(BF16)
        kbuf[0:QBLK, :] = kbuf[TB:TB + QBLK, :]
        vbuf[0:QBLK, :] = vbuf[TB:TB + QBLK, :]

    def mixer_c():
        xc = jnp.broadcast_to(prow(ccb_ref), (TB, W))
        xc = _causal_taps(xc, cbuf[...], ccw_ref, CONV_C, C_PAD)
        cbuf[0:C_PAD, :] = cbuf[TB:TB + C_PAD, :]
        xcb = xc.astype(BF16)
        rg = _sigmoid(_dot(xcb, cwr_ref[...]) + prow(cbr_ref))
        ig = _sigmoid(_dot(xcb, cwi_ref[...]) + prow(cbi_ref))
        yield
        nlam = -prow(lam_ref)
        softplus = jnp.maximum(nlam, 0.0) + jnp.log1p(jnp.exp(-jnp.abs(nlam)))
        a = jnp.exp2(rg * (softplus * (-LRU_C * LOG2E)))
        gap = 1.0 - a * a
        root = jnp.where(gap > 0.0, gap * lax.rsqrt(gap), 0.0)
        uu = root * (ig * xc)
        step = 1
        while step < TB:
            uu = a * _shift_rows(uu, step, 0.0) + uu
            a = a * _shift_rows(a, step, 1.0)
            step *= 2
        hs = uu + a * hcar[0:1, :]
        hcar[0:1, :] = hs[TB - 1:TB, :]
        shared["hs"] = hs
        ycat[:, 2 * W:3 * W] = (hs * gates[:, 2 * W:3 * W]).astype(BF16)

    def mixer_d():
        dq = pd[:, 0:LANES] * (D_KEY ** -0.5)
        dk = pd[:, LANES:2 * LANES]
        dv = pd[:, 2 * LANES:2 * LANES + W]
        z = _dot(plr[...].astype(BF16), dwup[...]) + prow(dbu_ref)
        yield
        lg = (jnp.minimum(z, 0.0) - jnp.log1p(jnp.exp2(jnp.abs(z) * -LOG2E))) * (LOG2E / GATE_TAU)
        bcum = None
        r = lg
        for p_ in range(2):
            part = r.astype(BF16)
            d_ = _dot(tri[...], part)
            bcum = d_ if bcum is None else bcum + d_
            r = r - part.astype(F32)
        yield
        lane = lax.broadcasted_iota(jnp.int32, (1, LANES), 1)
        lane256 = lax.broadcasted_iota(jnp.int32, (1, W), 1)
        ci = lax.broadcasted_iota(jnp.int32, (CHUNK, W), 0)
        cj = lax.broadcasted_iota(jnp.int32, (CHUNK, W), 1) % CHUNK
        causal = (ci >= cj).astype(F32)
        bdm = (lax.broadcasted_iota(jnp.int32, (W, LANES), 0) // HEAD
               == lax.broadcasted_iota(jnp.int32, (W, LANES), 1) // D_KEY).astype(F32)
        state = st[...]
        od_parts = []
        for c in range(TB // CHUNK):
            r0 = c * CHUNK
            bc = bcum[r0:r0 + CHUNK, :]
            bl = bc[CHUNK - 1:CHUNK, :]
            qt = (dq[r0:r0 + CHUNK, :] * jnp.exp2(bc)).astype(BF16)
            kc = dk[r0:r0 + CHUNK, :]
            kt = kc * jnp.exp2(-bc)
            ke = (kc * jnp.exp2(bl - bc)).astype(BF16)
            vc = dv[r0:r0 + CHUNK, :]
            kst = jnp.concatenate([kt * (lane // D_KEY == h).astype(F32) for h in range(D_HEADS)],
                                  axis=0).astype(BF16)
            att = (_dot_nt(qt, kst) * causal).astype(BF16)
            yield
            vbd = jnp.concatenate([vc * (lane256 // HEAD == h).astype(F32) for h in range(D_HEADS)],
                                  axis=0).astype(BF16)
            o_c = _dot(att, vbd) + _dot_nt(qt, state.astype(BF16))
            state = state * jnp.exp2(bl) + _dot_tn(vc.astype(BF16), ke) * bdm
            od_parts.append(o_c)
            yield
        st[...] = state
        od = jnp.concatenate(od_parts, axis=0)
        odn = od * lax.rsqrt(_dot_split(od * od, gmat, 1) + EPS) * prow(dng_ref)
        ycat[:, 3 * W:4 * W] = (odn * gates[:, 3 * W:4 * W]).astype(BF16)

    def out_proj_prev():
        for c0 in range(0, D_MODEL, OUT_SLAB):
            o_ref[0, :, c0:c0 + OUT_SLAB] = (xo_ref[0, :, c0:c0 + OUT_SLAB]
                                             + _dot(yprev[...], wout_ref[:, c0:c0 + OUT_SLAB]))
            yield

    @pl.when(s_idx < n_tiles)
    def main_step():
        nonlocal hb, gmat
        x = x_ref[0]
        ms = jnp.mean(x * x, axis=-1, keepdims=True)
        hb = (x * lax.rsqrt(ms + EPS) * prow(ng_ref)).astype(BF16)
        gmat = g256[...]
        gens = {"A": mixer_a(), "B": mixer_b(), "C": mixer_c(), "D": mixer_d(), "O": out_proj_prev(),
                "P": project()}
        for name in TRACE_ORDER:
            next(gens[name], None)
        running = [gens[k] for k in "PDABCO"]
        while running:
            running = [g for g in running if next(g, StopIteration) is not StopIteration]
        yprev[...] = ycat[...]

    @pl.when(s_idx == n_tiles)
    def last_step():
        for _ in out_proj_prev():
            pass


def _t5_bucket(dist):
    max_exact = N_BUCKETS // 2
    d = jnp.maximum(dist, 1).astype(F32)
    large = max_exact + (jnp.log(d / max_exact) / math.log(MAX_DISTANCE / max_exact)
                         * (N_BUCKETS - max_exact)).astype(jnp.int32)
    large = jnp.minimum(large, N_BUCKETS - 1)
    return jnp.where(dist < max_exact, dist, large)


def _layer_spec(layer, shape):
    return pl.BlockSpec((None,) + shape, lambda s: (layer,) + (0,) * len(shape))


def _whole(shape):
    return pl.BlockSpec(shape, lambda s: (0,) * len(shape))


def _layer_call(layer, tiles_per_seq, x, ng, win, wtail, wout, apw, cwr, cwi, dwup, aconv, vec256, ccw, dbu,
                dng, bqg, bkg, bucket, sinks, relb):
    n_tiles = x.shape[0]
    smem = pl.BlockSpec(memory_space=pltpu.SMEM)
    spec = functools.partial(_layer_spec, layer)
    acb, alg, alb, ccb, cbr, cbi, lam = vec256
    prev_tile = lambda s: (jnp.maximum(s - 1, 0), 0, 0)
    return pl.pallas_call(
        functools.partial(_layer_kernel, layer, tiles_per_seq, n_tiles),
        grid=(n_tiles + 1,),
        in_specs=[
            pl.BlockSpec((1, TB, D_MODEL), lambda s: (jnp.minimum(s, n_tiles - 1), 0, 0)),
            pl.BlockSpec((1, TB, D_MODEL), prev_tile),
            _whole((DEPTH, D_MODEL)), spec((D_MODEL, IN_COLS)), spec((D_MODEL, D_TAIL)),
            spec((D_MODEL, D_MODEL)), spec((W, W)), spec((W, W)), spec((W, W)),
            spec((GATE_RANK, LANES)), spec((CONV_A, W)),
            _whole((DEPTH, W)), _whole((DEPTH, W)), _whole((DEPTH, W)),
            spec((CONV_C, W)),
            _whole((DEPTH, W)), _whole((DEPTH, W)), _whole((DEPTH, W)), _whole((DEPTH, W)),
            _whole((DEPTH, LANES)),
            _whole((DEPTH, W)), _whole((DEPTH, W)), _whole((DEPTH, LANES)),
            _whole((QBLK, 2 * QBLK)),
            smem, smem,
        ],
        out_specs=pl.BlockSpec((1, TB, D_MODEL), prev_tile),
        out_shape=jax.ShapeDtypeStruct(x.shape, F32),
        scratch_shapes=[
            pltpu.VMEM((TB + A_PAD, W), F32),
            pltpu.VMEM((TB + C_PAD, W), F32),
            pltpu.VMEM((SUBLANES, W), F32),
            pltpu.VMEM((TB + QBLK, LANES), BF16),
            pltpu.VMEM((TB + QBLK, LANES), BF16),
            pltpu.VMEM((W, LANES), F32),
            pltpu.VMEM((TB, D_MODEL), BF16),
            pltpu.VMEM((4 * QBLK, 2 * QBLK), F32),
            pltpu.VMEM((W, W), BF16),
            pltpu.VMEM((TB, TB), BF16),
            pltpu.VMEM((LANES, LANES), BF16),
            pltpu.VMEM((TB, D_MODEL), F32),
            pltpu.VMEM((TB, B_QKV), F32),
            pltpu.VMEM((TB, D_QKV), F32),
            pltpu.VMEM((TB, LANES), F32),
            pltpu.VMEM((TB, D_MODEL), BF16),
        ],
        compiler_params=pltpu.CompilerParams(
            dimension_semantics=("arbitrary",),
            vmem_limit_bytes=VMEM_LIMIT_BYTES,
        ),
        name="hybrid_layer",
    )(x, x, ng, win, wtail, wout, apw, cwr, cwi, dwup, aconv, acb, alg, alb, ccw, ccb, cbr, cbi, lam, dbu,
      dng, bqg, bkg, bucket, sinks, relb)


def _block_diag(blocks):
    depth, n, c, _ = blocks.shape
    eye = jnp.eye(n, dtype=blocks.dtype)
    return (eye[None, :, None, :, None] * blocks[:, :, :, None, :]).reshape(depth, n * c, n * c)


def kernel(x, norm_g, w_in, a_conv_w, a_conv_b, a_ln_g, a_ln_b, a_pw, b_q_g, b_k_g, b_sinks, rel_bias,
           c_conv_w, c_conv_b, c_w_r, c_b_r, c_w_i, c_b_i, c_lambda, d_w_up, d_b_up, d_norm_g, w_out):
    dist = jnp.arange(QBLK)[:, None] + QBLK - jnp.arange(2 * QBLK)[None, :]
    bucket = _t5_bucket(jnp.clip(dist, 0, None)).astype(jnp.int32)
    wtail = jnp.concatenate([
        w_in[:, :, D_TAIL_SRC + GATE_RANK:IN_COLS], w_in[:, :, D_TAIL_SRC:D_TAIL_SRC + GATE_RANK],
        jnp.zeros((DEPTH, D_MODEL, D_TAIL - (IN_COLS - D_TAIL_SRC)), F32)], axis=2).astype(BF16)
    vec256 = (a_conv_b, a_ln_g, a_ln_b, c_conv_b, c_b_r, c_b_i, c_lambda)
    args = (norm_g, w_in.astype(BF16), wtail, w_out.astype(BF16), a_pw.astype(BF16),
            _block_diag(c_w_r).astype(BF16), _block_diag(c_w_i).astype(BF16), d_w_up, a_conv_w,
            vec256, c_conv_w, d_b_up, jnp.tile(d_norm_g, (1, 4)), jnp.tile(b_q_g, (1, 4)),
            jnp.tile(b_k_g, (1, 2)), bucket, b_sinks.reshape(-1), rel_bias.reshape(-1))
    bsz, seq, _ = x.shape
    assert seq % TB == 0
    h = x.reshape(bsz * seq // TB, TB, D_MODEL)
    for l in range(DEPTH):
        h = _layer_call(l, seq // TB, h, *args)
    return h.reshape(bsz, seq, D_MODEL)
```

```python
import functools
import math

import jax
import jax.numpy as jnp
from jax import lax
from jax.experimental import pallas as pl
from jax.experimental.pallas import tpu as pltpu

F32 = jnp.float32
BF16 = jnp.bfloat16

D_MODEL = 1024
DEPTH = 4
W = 256
EPS = 1e-6
CONV_A = 31
A_PAD = 32
HEAD = 64
WINDOW = 128
QBLK = 128
N_BUCKETS = 32
MAX_DISTANCE = 128
CONV_C = 4
C_PAD = 8
LRU_C = 8.0
D_HEADS = 4
D_KEY = 32
GATE_RANK = 16
GATE_TAU = 16.0
CHUNK = 64
SUBLANES = 8
LANES = 128
VMEM_LIMIT_BYTES = 48 * 1024 * 1024
LOG2E = math.log2(math.e)

TB = 512
TRACE_ORDER = ("O P A P P C P P O B D P C O D B P D D B D D O P D D D D B D D D D B D D D D B P A B B B A A "
               "D B").split()
OUT_SLAB = 256

A_OFF = 0
B_OFF = 768
B_QKV = 512
C_OFF = 1536
D_OFF, D_QKV = 2048, 512
D_TAIL_SRC = D_OFF + D_QKV
D_TAIL = 384
IN_COLS = 2832


def _dot(a, b):
    return jnp.dot(a, b, preferred_element_type=F32)


def _dot_nt(a, b):
    return lax.dot_general(a, b, (((1,), (1,)), ((), ())), preferred_element_type=F32)


def _dot_tn(a, b):
    return lax.dot_general(a, b, (((0,), (0,)), ((), ())), preferred_element_type=F32)


def _dot_split(x, w_bf, passes):
    acc = None
    r = x
    for p in range(passes):
        part = r.astype(BF16)
        d = _dot(part, w_bf)
        acc = d if acc is None else acc + d
        if p + 1 < passes:
            r = r - part.astype(F32)
    return acc


def _sigmoid(x):
    return 1.0 / (1.0 + jnp.exp2(x * -LOG2E))


def _silu(x):
    return x * _sigmoid(x)


def _causal_taps(acc, buf, w_ref, n_taps, pad):
    rows = buf.shape[0]
    for res in range(SUBLANES):
        shifted = None
        for j in range(n_taps):
            off = pad - (n_taps - 1) + j
            if off % SUBLANES != res:
                continue
            if shifted is None:
                shifted = buf if res == 0 else pltpu.roll(buf, rows - res, axis=0)
            base = off - res
            acc = acc + w_ref[j:j + 1, :] * shifted[base:base + TB]
    return acc


def _shift_rows(x, step, fill):
    if step % SUBLANES == 0:
        return jnp.concatenate([jnp.full((step, x.shape[1]), fill, F32), x[0:x.shape[0] - step]], axis=0)
    rolled = pltpu.roll(x, step, axis=0)
    row = lax.broadcasted_iota(jnp.int32, (SUBLANES, x.shape[1]), 0)
    head = jnp.where(row < step, fill, rolled[0:SUBLANES])
    return jnp.concatenate([head, rolled[SUBLANES:]], axis=0)


def _layer_kernel(layer, tiles_per_seq, n_tiles,
                  x_ref, xo_ref, ng_ref, win_ref, wtail_ref, wout_ref, apw_ref, cwr_ref, cwi_ref, dwup_ref, aconv_ref,
                  acb_ref, alg_ref, alb_ref, ccw_ref, ccb_ref, cbr_ref, cbi_ref, lam_ref, dbu_ref,
                  dng_ref, bqg_ref, bkg_ref, bucket_ref, sinks_ref, relb_ref,
                  o_ref,
                  ubuf, cbuf, hcar, kbuf, vbuf, st, ycat, biasm, g256, tri, dwup, gates, pb, pd, plr, yprev):
    s_idx = pl.program_id(0)
    seq_start = (s_idx % tiles_per_seq) == 0
    prow = lambda ref: ref[layer:layer + 1, :]

    @pl.when(s_idx == 0)
    def _build_tables():
        yprev[...] = jnp.zeros((TB, D_MODEL), BF16)
        ri = lax.broadcasted_iota(jnp.int32, (W, W), 0) // HEAD
        ci = lax.broadcasted_iota(jnp.int32, (W, W), 1) // HEAD
        g256[...] = jnp.where(ri == ci, 1.0 / HEAD, 0.0).astype(BF16)
        rt = lax.broadcasted_iota(jnp.int32, (TB, TB), 0)
        ct = lax.broadcasted_iota(jnp.int32, (TB, TB), 1)
        tri[...] = jnp.where((rt // CHUNK == ct // CHUNK) & (rt >= ct), 1.0, 0.0).astype(BF16)
        dwup[...] = jnp.zeros((LANES, LANES), BF16)
        dwup[0:GATE_RANK, :] = dwup_ref[...].astype(BF16)
        bucket = bucket_ref[...]
        qi = lax.broadcasted_iota(jnp.int32, (QBLK, 2 * QBLK), 0)
        kj = lax.broadcasted_iota(jnp.int32, (QBLK, 2 * QBLK), 1)
        dist = qi + QBLK - kj
        valid = (dist >= 0) & (dist < WINDOW)
        for h in range(4):
            acc = jnp.zeros((QBLK, 2 * QBLK), F32)
            for bk in range(N_BUCKETS):
                acc = jnp.where(bucket == bk, relb_ref[bk * 4 + h], acc)
            biasm[h * QBLK:(h + 1) * QBLK, :] = jnp.where(valid, acc * LOG2E, -jnp.inf)

    @pl.when(seq_start)
    def _reset_state():
        ubuf[0:A_PAD, :] = jnp.zeros((A_PAD, W), F32)
        cbuf[0:C_PAD, :] = jnp.zeros((C_PAD, W), F32)
        hcar[...] = jnp.zeros((SUBLANES, W), F32)
        kbuf[0:QBLK, :] = jnp.zeros((QBLK, LANES), BF16)
        vbuf[0:QBLK, :] = jnp.zeros((QBLK, LANES), BF16)
        st[...] = jnp.zeros((W, LANES), F32)

    hb = gmat = None

    proj = lambda off, width: _dot(hb, win_ref[:, off:off + width])
    shared = {}

    def project():
        pav = proj(A_OFF, 2 * W)
        ubuf[A_PAD:A_PAD + TB, :] = pav[:, 0:W] * _sigmoid(pav[:, W:2 * W])
        yield
        cbuf[C_PAD:C_PAD + TB, :] = proj(C_OFF, W)
        yield
        pb[...] = proj(B_OFF, B_QKV)
        yield
        pd[...] = proj(D_OFF, D_QKV)
        yield
        plr[...] = _dot(hb, wtail_ref[:, W:W + LANES])
        yield
        gates[:, 2 * W:3 * W] = _silu(proj(C_OFF + W, W))
        yield
        gates[:, 0:W] = _silu(proj(A_OFF + 2 * W, W))
        yield
        gates[:, W:2 * W] = _silu(proj(B_OFF + B_QKV, W))
        yield
        gates[:, 3 * W:4 * W] = _silu(_dot(hb, wtail_ref[:, 0:W]))


    def mixer_a():
        conv = jnp.broadcast_to(prow(acb_ref), (TB, W))
        conv = _causal_taps(conv, ubuf[...], aconv_ref, CONV_A, A_PAD)
        ubuf[0:A_PAD, :] = ubuf[TB:TB + A_PAD, :]
        yield
        mu = _dot_split(conv, gmat, 2)
        yield
        dc = conv - mu
        var = _dot_split(dc * dc, gmat, 1)
        yield
        un = dc * lax.rsqrt(var + EPS) * prow(alg_ref) + prow(alb_ref)
        ya = _dot(_silu(un).astype(BF16), apw_ref[...]) * gates[:, 0:W]
        ycat[:, 0:W] = ya.astype(BF16)

    def mixer_b():
        q = pb[:, 0:W]
        k = pb[:, W:W + LANES]
        v = pb[:, W + LANES:W + 2 * LANES]
        qn = q * lax.rsqrt(_dot_split(q * q, gmat, 1) + EPS)
        kn = k * lax.rsqrt(_dot_split(k * k, gmat[0:LANES, 0:LANES], 1) + EPS)
        yield
        kbuf[QBLK:QBLK + TB, :] = kn.astype(BF16)
        vbuf[QBLK:QBLK + TB, :] = v.astype(BF16)
        lane = lax.broadcasted_iota(jnp.int32, (1, LANES), 1)
        lanem = lane < HEAD
        qk_gain = prow(bqg_ref)[:, 0:LANES] * prow(bkg_ref) * (HEAD ** -0.5 * LOG2E)
        hs_bits = pltpu.bitcast(shared["hs"][TB - SUBLANES:TB, LANES:2 * LANES], jnp.int32)
        zero = (hs_bits & jnp.minimum(s_idx, 0))[0:1, :]
        qk_gain = pltpu.bitcast(pltpu.bitcast(qk_gain, jnp.int32) | zero, F32)
        lo = jnp.where(lanem, qk_gain, 0.0)
        hi = jnp.where(lanem, 0.0, qk_gain)
        first_tile_mask = jnp.where(seq_start, -jnp.inf, 0.0)

        def softmax_rows(sh, h, bi):
            if bi == 0:
                sh = jnp.concatenate([sh[:, 0:QBLK] + first_tile_mask, sh[:, QBLK:]], axis=1)
            sink = sinks_ref[layer * 4 + h] * LOG2E
            m = jnp.maximum(jnp.max(sh, axis=-1, keepdims=True), sink)
            p = jnp.exp2(sh - m)
            den = jnp.sum(p, axis=-1, keepdims=True) + jnp.exp2(sink - m)
            return p.astype(BF16), den

        for bi in range(TB // QBLK):
            r0 = bi * QBLK
            qa = qn[r0:r0 + QBLK, 0:LANES]
            qb = qn[r0:r0 + QBLK, LANES:2 * LANES]
            qst = jnp.concatenate([qa * lo, pltpu.roll(qa, HEAD, axis=1) * lo,
                                   pltpu.roll(qb, HEAD, axis=1) * hi, qb * hi], axis=0).astype(BF16)
            s = _dot_nt(qst, kbuf[r0:r0 + 2 * QBLK, :]) + biasm[...]
            yield
            pden = [softmax_rows(s[h * QBLK:(h + 1) * QBLK, :], h, bi) for h in range(4)]
            ov = _dot(jnp.concatenate([t[0] for t in pden], axis=0), vbuf[r0:r0 + 2 * QBLK, :])
            yield
            oh = [ov[h * QBLK:(h + 1) * QBLK, :] / pden[h][1] for h in range(4)]
            y01 = jnp.where(lanem, oh[0], pltpu.roll(oh[1], HEAD, axis=1))
            y23 = jnp.where(lanem, pltpu.roll(oh[2], HEAD, axis=1), oh[3])
            ycat[r0:r0 + QBLK, W:2 * W] = (jnp.concatenate([y01, y23], axis=1)
                                           * gates[r0:r0 + QBLK, W:2 * W]).astype(BF16)
        kbuf[0:QBLK, :] = kbuf[TB:TB + QBLK, :]
        vbuf[0:QBLK, :] = vbuf[TB:TB + QBLK, :]

    def mixer_c():
        xc = jnp.broadcast_to(prow(ccb_ref), (TB, W))
        xc = _causal_taps(xc, cbuf[...], ccw_ref, CONV_C, C_PAD)
        cbuf[0:C_PAD, :] = cbuf[TB:TB + C_PAD, :]
        xcb = xc.astype(BF16)
        rg = _sigmoid(_dot(xcb, cwr_ref[...]) + prow(cbr_ref))
        ig = _sigmoid(_dot(xcb, cwi_ref[...]) + prow(cbi_ref))
        yield
        nlam = -prow(lam_ref)
        softplus = jnp.maximum(nlam, 0.0) + jnp.log1p(jnp.exp(-jnp.abs(nlam)))
        a = jnp.exp2(rg * (softplus * (-LRU_C * LOG2E)))
        gap = 1.0 - a * a
        root = jnp.where(gap > 0.0, gap * lax.rsqrt(gap), 0.0)
        uu = root * (ig * xc)
        step = 1
        while step < TB:
            uu = a * _shift_rows(uu, step, 0.0) + uu
            a = a * _shift_rows(a, step, 1.0)
            step *= 2
        hs = uu + a * hcar[0:1, :]
        hcar[0:1, :] = hs[TB - 1:TB, :]
        shared["hs"] = hs
        ycat[:, 2 * W:3 * W] = (hs * gates[:, 2 * W:3 * W]).astype(BF16)

    def mixer_d():
        dq = pd[:, 0:LANES] * (D_KEY ** -0.5)
        dk = pd[:, LANES:2 * LANES]
        dv = pd[:, 2 * LANES:2 * LANES + W]
        z = _dot(plr[...].astype(BF16), dwup[...]) + prow(dbu_ref)
        yield
        lg = (jnp.minimum(z, 0.0) - jnp.log1p(jnp.exp2(jnp.abs(z) * -LOG2E))) * (LOG2E / GATE_TAU)
        lg_hi = lg.astype(BF16)
        lg_lo = (lg - lg_hi.astype(F32)).astype(BF16)
        both = _dot(tri[...], jnp.concatenate([lg_hi, lg_lo], axis=1))
        bcum = both[:, 0:LANES] + both[:, LANES:2 * LANES]
        yield
        lane = lax.broadcasted_iota(jnp.int32, (1, LANES), 1)
        lane256 = lax.broadcasted_iota(jnp.int32, (1, W), 1)
        ci = lax.broadcasted_iota(jnp.int32, (CHUNK, W), 0)
        cj = lax.broadcasted_iota(jnp.int32, (CHUNK, W), 1) % CHUNK
        causal = (ci >= cj).astype(F32)
        bdm = (lax.broadcasted_iota(jnp.int32, (W, LANES), 0) // HEAD
               == lax.broadcasted_iota(jnp.int32, (W, LANES), 1) // D_KEY).astype(F32)
        state = st[...]
        od_parts = []
        for c in range(TB // CHUNK):
            r0 = c * CHUNK
            bc = bcum[r0:r0 + CHUNK, :]
            bl = bc[CHUNK - 1:CHUNK, :]
            qt = (dq[r0:r0 + CHUNK, :] * jnp.exp2(bc)).astype(BF16)
            kc = dk[r0:r0 + CHUNK, :]
            kt = kc * jnp.exp2(-bc)
            ke = (kc * jnp.exp2(bl - bc)).astype(BF16)
            vc = dv[r0:r0 + CHUNK, :]
            kst = jnp.concatenate([kt * (lane // D_KEY == h).astype(F32) for h in range(D_HEADS)],
                                  axis=0).astype(BF16)
            att = (_dot_nt(qt, kst) * causal).astype(BF16)
            yield
            vbd = jnp.concatenate([vc * (lane256 // HEAD == h).astype(F32) for h in range(D_HEADS)],
                                  axis=0).astype(BF16)
            o_c = _dot(att, vbd) + _dot_nt(qt, state.astype(BF16))
            state = state * jnp.exp2(bl) + _dot_tn(vc.astype(BF16), ke) * bdm
            od_parts.append(o_c)
            yield
        st[...] = state
        od = jnp.concatenate(od_parts, axis=0)
        odn = od * lax.rsqrt(_dot_split(od * od, gmat, 1) + EPS) * prow(dng_ref)
        ycat[:, 3 * W:4 * W] = (odn * gates[:, 3 * W:4 * W]).astype(BF16)

    def out_proj_prev():
        for c0 in range(0, D_MODEL, OUT_SLAB):
            o_ref[0, :, c0:c0 + OUT_SLAB] = (xo_ref[0, :, c0:c0 + OUT_SLAB]
                                             + _dot(yprev[...], wout_ref[:, c0:c0 + OUT_SLAB]))
            yield

    @pl.when(s_idx < n_tiles)
    def main_step():
        nonlocal hb, gmat
        x = x_ref[0]
        ms = jnp.mean(x * x, axis=-1, keepdims=True)
        hb = (x * lax.rsqrt(ms + EPS) * prow(ng_ref)).astype(BF16)
        gmat = g256[...]
        gens = {"A": mixer_a(), "B": mixer_b(), "C": mixer_c(), "D": mixer_d(), "O": out_proj_prev(),
                "P": project()}
        for name in TRACE_ORDER:
            next(gens[name], None)
        running = [gens[k] for k in "PDABCO"]
        while running:
            running = [g for g in running if next(g, StopIteration) is not StopIteration]
        yprev[...] = ycat[...]

    @pl.when(s_idx == n_tiles)
    def last_step():
        for _ in out_proj_prev():
            pass


def _t5_bucket(dist):
    max_exact = N_BUCKETS // 2
    d = jnp.maximum(dist, 1).astype(F32)
    large = max_exact + (jnp.log(d / max_exact) / math.log(MAX_DISTANCE / max_exact)
                         * (N_BUCKETS - max_exact)).astype(jnp.int32)
    large = jnp.minimum(large, N_BUCKETS - 1)
    return jnp.where(dist < max_exact, dist, large)


def _layer_spec(layer, shape):
    return pl.BlockSpec((None,) + shape, lambda s: (layer,) + (0,) * len(shape))


def _whole(shape):
    return pl.BlockSpec(shape, lambda s: (0,) * len(shape))


def _layer_call(layer, tiles_per_seq, x, ng, win, wtail, wout, apw, cwr, cwi, dwup, aconv, vec256, ccw, dbu,
                dng, bqg, bkg, bucket, sinks, relb):
    n_tiles = x.shape[0]
    smem = pl.BlockSpec(memory_space=pltpu.SMEM)
    spec = functools.partial(_layer_spec, layer)
    acb, alg, alb, ccb, cbr, cbi, lam = vec256
    prev_tile = lambda s: (jnp.maximum(s - 1, 0), 0, 0)
    return pl.pallas_call(
        functools.partial(_layer_kernel, layer, tiles_per_seq, n_tiles),
        grid=(n_tiles + 1,),
        in_specs=[
            pl.BlockSpec((1, TB, D_MODEL), lambda s: (jnp.minimum(s, n_tiles - 1), 0, 0)),
            pl.BlockSpec((1, TB, D_MODEL), prev_tile),
            _whole((DEPTH, D_MODEL)), spec((D_MODEL, IN_COLS)), spec((D_MODEL, D_TAIL)),
            spec((D_MODEL, D_MODEL)), spec((W, W)), spec((W, W)), spec((W, W)),
            spec((GATE_RANK, LANES)), spec((CONV_A, W)),
            _whole((DEPTH, W)), _whole((DEPTH, W)), _whole((DEPTH, W)),
            spec((CONV_C, W)),
            _whole((DEPTH, W)), _whole((DEPTH, W)), _whole((DEPTH, W)), _whole((DEPTH, W)),
            _whole((DEPTH, LANES)),
            _whole((DEPTH, W)), _whole((DEPTH, W)), _whole((DEPTH, LANES)),
            _whole((QBLK, 2 * QBLK)),
            smem, smem,
        ],
        out_specs=pl.BlockSpec((1, TB, D_MODEL), prev_tile),
        out_shape=jax.ShapeDtypeStruct(x.shape, F32),
        scratch_shapes=[
            pltpu.VMEM((TB + A_PAD, W), F32),
            pltpu.VMEM((TB + C_PAD, W), F32),
            pltpu.VMEM((SUBLANES, W), F32),
            pltpu.VMEM((TB + QBLK, LANES), BF16),
            pltpu.VMEM((TB + QBLK, LANES), BF16),
            pltpu.VMEM((W, LANES), F32),
            pltpu.VMEM((TB, D_MODEL), BF16),
            pltpu.VMEM((4 * QBLK, 2 * QBLK), F32),
            pltpu.VMEM((W, W), BF16),
            pltpu.VMEM((TB, TB), BF16),
            pltpu.VMEM((LANES, LANES), BF16),
            pltpu.VMEM((TB, D_MODEL), F32),
            pltpu.VMEM((TB, B_QKV), F32),
            pltpu.VMEM((TB, D_QKV), F32),
            pltpu.VMEM((TB, LANES), F32),
            pltpu.VMEM((TB, D_MODEL), BF16),
        ],
        compiler_params=pltpu.CompilerParams(
            dimension_semantics=("arbitrary",),
            vmem_limit_bytes=VMEM_LIMIT_BYTES,
        ),
        name="hybrid_layer",
    )(x, x, ng, win, wtail, wout, apw, cwr, cwi, dwup, aconv, acb, alg, alb, ccw, ccb, cbr, cbi, lam, dbu,
      dng, bqg, bkg, bucket, sinks, relb)


def _block_diag(blocks):
    depth, n, c, _ = blocks.shape
    eye = jnp.eye(n, dtype=blocks.dtype)
    return (eye[None, :, None, :, None] * blocks[:, :, :, None, :]).reshape(depth, n * c, n * c)


def kernel(x, norm_g, w_in, a_conv_w, a_conv_b, a_ln_g, a_ln_b, a_pw, b_q_g, b_k_g, b_sinks, rel_bias,
           c_conv_w, c_conv_b, c_w_r, c_b_r, c_w_i, c_b_i, c_lambda, d_w_up, d_b_up, d_norm_g, w_out):
    dist = jnp.arange(QBLK)[:, None] + QBLK - jnp.arange(2 * QBLK)[None, :]
    bucket = _t5_bucket(jnp.clip(dist, 0, None)).astype(jnp.int32)
    wtail = jnp.concatenate([
        w_in[:, :, D_TAIL_SRC + GATE_RANK:IN_COLS], w_in[:, :, D_TAIL_SRC:D_TAIL_SRC + GATE_RANK],
        jnp.zeros((DEPTH, D_MODEL, D_TAIL - (IN_COLS - D_TAIL_SRC)), F32)], axis=2).astype(BF16)
    vec256 = (a_conv_b, a_ln_g, a_ln_b, c_conv_b, c_b_r, c_b_i, c_lambda)
    args = (norm_g, w_in.astype(BF16), wtail, w_out.astype(BF16), a_pw.astype(BF16),
            _block_diag(c_w_r).astype(BF16), _block_diag(c_w_i).astype(BF16), d_w_up, a_conv_w,
            vec256, c_conv_w, d_b_up, jnp.tile(d_norm_g, (1, 4)), jnp.tile(b_q_g, (1, 4)),
            jnp.tile(b_k_g, (1, 2)), bucket, b_sinks.reshape(-1), rel_bias.reshape(-1))
    bsz, seq, _ = x.shape
    assert seq % TB == 0
    h = x.reshape(bsz * seq // TB, TB, D_MODEL)
    for l in range(DEPTH):
        h = _layer_call(l, seq // TB, h, *args)
    return h.reshape(bsz, seq, D_MODEL)
```

```python
import functools
import math

import jax
import jax.numpy as jnp
from jax import lax
from jax.experimental import pallas as pl
from jax.experimental.pallas import tpu as pltpu

F32 = jnp.float32
BF16 = jnp.bfloat16

D_MODEL = 1024
DEPTH = 4
W = 256
EPS = 1e-6
CONV_A = 31
A_PAD = 32
HEAD = 64
WINDOW = 128
QBLK = 128
N_BUCKETS = 32
MAX_DISTANCE = 128
CONV_C = 4
C_PAD = 8
LRU_C = 8.0
D_HEADS = 4
D_KEY = 32
GATE_RANK = 16
GATE_TAU = 16.0
CHUNK = 64
SUBLANES = 8
LANES = 128
VMEM_LIMIT_BYTES = 48 * 1024 * 1024
LOG2E = math.log2(math.e)

TB = 512
TRACE_ORDER = ("O P A P P C P P O B D P C O D B P D D B D D O P D D D D B D D D D B D D D D B P A B B B A A "
               "D B").split()
OUT_SLAB = 256
CUM_ROWS = 256

A_OFF = 0
B_OFF = 768
B_QKV = 512
C_OFF = 1536
D_OFF, D_QKV = 2048, 512
D_TAIL_SRC = D_OFF + D_QKV
D_TAIL = 384
IN_COLS = 2832


def _dot(a, b):
    return jnp.dot(a, b, preferred_element_type=F32)


def _dot_nt(a, b):
    return lax.dot_general(a, b, (((1,), (1,)), ((), ())), preferred_element_type=F32)


def _dot_tn(a, b):
    return lax.dot_general(a, b, (((0,), (0,)), ((), ())), preferred_element_type=F32)


def _dot_split(x, w_bf, passes):
    acc = None
    r = x
    for p in range(passes):
        part = r.astype(BF16)
        d = _dot(part, w_bf)
        acc = d if acc is None else acc + d
        if p + 1 < passes:
            r = r - part.astype(F32)
    return acc


def _sigmoid(x):
    return 1.0 / (1.0 + jnp.exp2(x * -LOG2E))


def _silu(x):
    return x * _sigmoid(x)


def _causal_taps(acc, buf, w_ref, n_taps, pad):
    rows = buf.shape[0]
    for res in range(SUBLANES):
        shifted = None
        for j in range(n_taps):
            off = pad - (n_taps - 1) + j
            if off % SUBLANES != res:
                continue
            if shifted is None:
                shifted = buf if res == 0 else pltpu.roll(buf, rows - res, axis=0)
            base = off - res
            acc = acc + w_ref[j:j + 1, :] * shifted[base:base + TB]
    return acc


def _shift_rows(x, step, fill):
    if step % SUBLANES == 0:
        return jnp.concatenate([jnp.full((step, x.shape[1]), fill, F32), x[0:x.shape[0] - step]], axis=0)
    rolled = pltpu.roll(x, step, axis=0)
    row = lax.broadcasted_iota(jnp.int32, (SUBLANES, x.shape[1]), 0)
    head = jnp.where(row < step, fill, rolled[0:SUBLANES])
    return jnp.concatenate([head, rolled[SUBLANES:]], axis=0)


def _layer_kernel(layer, tiles_per_seq, n_tiles,
                  x_ref, xo_ref, ng_ref, win_ref, wtail_ref, wout_ref, apw_ref, cwr_ref, cwi_ref, dwup_ref, aconv_ref,
                  acb_ref, alg_ref, alb_ref, ccw_ref, ccb_ref, cbr_ref, cbi_ref, lam_ref, dbu_ref,
                  dng_ref, bqg_ref, bkg_ref, bucket_ref, sinks_ref, relb_ref,
                  o_ref,
                  ubuf, cbuf, hcar, kbuf, vbuf, st, ycat, biasm, g256, tri, dwup, gates, pb, pd, plr, yprev):
    s_idx = pl.program_id(0)
    seq_start = (s_idx % tiles_per_seq) == 0
    prow = lambda ref: ref[layer:layer + 1, :]

    @pl.when(s_idx == 0)
    def _build_tables():
        yprev[...] = jnp.zeros((TB, D_MODEL), BF16)
        ri = lax.broadcasted_iota(jnp.int32, (W, W), 0) // HEAD
        ci = lax.broadcasted_iota(jnp.int32, (W, W), 1) // HEAD
        g256[...] = jnp.where(ri == ci, 1.0 / HEAD, 0.0).astype(BF16)
        rt = lax.broadcasted_iota(jnp.int32, (CUM_ROWS, CUM_ROWS), 0)
        ct = lax.broadcasted_iota(jnp.int32, (CUM_ROWS, CUM_ROWS), 1)
        tri[...] = jnp.where((rt // CHUNK == ct // CHUNK) & (rt >= ct), 1.0, 0.0).astype(BF16)
        dwup[...] = jnp.zeros((LANES, LANES), BF16)
        dwup[0:GATE_RANK, :] = dwup_ref[...].astype(BF16)
        bucket = bucket_ref[...]
        qi = lax.broadcasted_iota(jnp.int32, (QBLK, 2 * QBLK), 0)
        kj = lax.broadcasted_iota(jnp.int32, (QBLK, 2 * QBLK), 1)
        dist = qi + QBLK - kj
        valid = (dist >= 0) & (dist < WINDOW)
        for h in range(4):
            acc = jnp.zeros((QBLK, 2 * QBLK), F32)
            for bk in range(N_BUCKETS):
                acc = jnp.where(bucket == bk, relb_ref[bk * 4 + h], acc)
            biasm[h * QBLK:(h + 1) * QBLK, :] = jnp.where(valid, acc * LOG2E, -jnp.inf)

    @pl.when(seq_start)
    def _reset_state():
        ubuf[0:A_PAD, :] = jnp.zeros((A_PAD, W), F32)
        cbuf[0:C_PAD, :] = jnp.zeros((C_PAD, W), F32)
        hcar[...] = jnp.zeros((SUBLANES, W), F32)
        kbuf[0:QBLK, :] = jnp.zeros((QBLK, LANES), BF16)
        vbuf[0:QBLK, :] = jnp.zeros((QBLK, LANES), BF16)
        st[...] = jnp.zeros((W, LANES), F32)

    hb = gmat = None

    proj = lambda off, width: _dot(hb, win_ref[:, off:off + width])
    shared = {}

    def project():
        pav = proj(A_OFF, 2 * W)
        ubuf[A_PAD:A_PAD + TB, :] = pav[:, 0:W] * _sigmoid(pav[:, W:2 * W])
        yield
        cbuf[C_PAD:C_PAD + TB, :] = proj(C_OFF, W)
        yield
        pb[...] = proj(B_OFF, B_QKV)
        yield
        pd[...] = proj(D_OFF, D_QKV)
        yield
        plr[...] = _dot(hb, wtail_ref[:, W:W + LANES])
        yield
        gates[:, 2 * W:3 * W] = _silu(proj(C_OFF + W, W))
        yield
        gates[:, 0:W] = _silu(proj(A_OFF + 2 * W, W))
        yield
        gates[:, W:2 * W] = _silu(proj(B_OFF + B_QKV, W))
        yield
        gates[:, 3 * W:4 * W] = _silu(_dot(hb, wtail_ref[:, 0:W]))


    def mixer_a():
        conv = jnp.broadcast_to(prow(acb_ref), (TB, W))
        conv = _causal_taps(conv, ubuf[...], aconv_ref, CONV_A, A_PAD)
        ubuf[0:A_PAD, :] = ubuf[TB:TB + A_PAD, :]
        yield
        mu = _dot_split(conv, gmat, 2)
        yield
        dc = conv - mu
        var = _dot_split(dc * dc, gmat, 1)
        yield
        un = dc * lax.rsqrt(var + EPS) * prow(alg_ref) + prow(alb_ref)
        ya = _dot(_silu(un).astype(BF16), apw_ref[...]) * gates[:, 0:W]
        ycat[:, 0:W] = ya.astype(BF16)

    def mixer_b():
        q = pb[:, 0:W]
        k = pb[:, W:W + LANES]
        v = pb[:, W + LANES:W + 2 * LANES]
        qn = q * lax.rsqrt(_dot_split(q * q, gmat, 1) + EPS)
        kn = k * lax.rsqrt(_dot_split(k * k, gmat[0:LANES, 0:LANES], 1) + EPS)
        yield
        kbuf[QBLK:QBLK + TB, :] = kn.astype(BF16)
        vbuf[QBLK:QBLK + TB, :] = v.astype(BF16)
        lane = lax.broadcasted_iota(jnp.int32, (1, LANES), 1)
        lanem = lane < HEAD
        qk_gain = prow(bqg_ref)[:, 0:LANES] * prow(bkg_ref) * (HEAD ** -0.5 * LOG2E)
        hs_bits = pltpu.bitcast(shared["hs"][TB - SUBLANES:TB, LANES:2 * LANES], jnp.int32)
        zero = (hs_bits & jnp.minimum(s_idx, 0))[0:1, :]
        qk_gain = pltpu.bitcast(pltpu.bitcast(qk_gain, jnp.int32) | zero, F32)
        lo = jnp.where(lanem, qk_gain, 0.0)
        hi = jnp.where(lanem, 0.0, qk_gain)
        first_tile_mask = jnp.where(seq_start, -jnp.inf, 0.0)

        def softmax_rows(sh, h, bi):
            if bi == 0:
                sh = jnp.concatenate([sh[:, 0:QBLK] + first_tile_mask, sh[:, QBLK:]], axis=1)
            sink = sinks_ref[layer * 4 + h] * LOG2E
            m = jnp.maximum(jnp.max(sh, axis=-1, keepdims=True), sink)
            p = jnp.exp2(sh - m)
            den = jnp.sum(p, axis=-1, keepdims=True) + jnp.exp2(sink - m)
            return p.astype(BF16), den

        for bi in range(TB // QBLK):
            r0 = bi * QBLK
            qa = qn[r0:r0 + QBLK, 0:LANES]
            qb = qn[r0:r0 + QBLK, LANES:2 * LANES]
            qst = jnp.concatenate([qa * lo, pltpu.roll(qa, HEAD, axis=1) * lo,
                                   pltpu.roll(qb, HEAD, axis=1) * hi, qb * hi], axis=0).astype(BF16)
            s = _dot_nt(qst, kbuf[r0:r0 + 2 * QBLK, :]) + biasm[...]
            yield
            pden = [softmax_rows(s[h * QBLK:(h + 1) * QBLK, :], h, bi) for h in range(4)]
            ov = _dot(jnp.concatenate([t[0] for t in pden], axis=0), vbuf[r0:r0 + 2 * QBLK, :])
            yield
            oh = [ov[h * QBLK:(h + 1) * QBLK, :] / pden[h][1] for h in range(4)]
            y01 = jnp.where(lanem, oh[0], pltpu.roll(oh[1], HEAD, axis=1))
            y23 = jnp.where(lanem, pltpu.roll(oh[2], HEAD, axis=1), oh[3])
            ycat[r0:r0 + QBLK, W:2 * W] = (jnp.concatenate([y01, y23], axis=1)
                                           * gates[r0:r0 + QBLK, W:2 * W]).astype(BF16)
        kbuf[0:QBLK, :] = kbuf[TB:TB + QBLK, :]
        vbuf[0:QBLK, :] = vbuf[TB:TB + QBLK, :]

    def mixer_c():
        xc = jnp.broadcast_to(prow(ccb_ref), (TB, W))
        xc = _causal_taps(xc, cbuf[...], ccw_ref, CONV_C, C_PAD)
        cbuf[0:C_PAD, :] = cbuf[TB:TB + C_PAD, :]
        xcb = xc.astype(BF16)
        rg = _sigmoid(_dot(xcb, cwr_ref[...]) + prow(cbr_ref))
        ig = _sigmoid(_dot(xcb, cwi_ref[...]) + prow(cbi_ref))
        yield
        nlam = -prow(lam_ref)
        softplus = jnp.maximum(nlam, 0.0) + jnp.log1p(jnp.exp(-jnp.abs(nlam)))
        a = jnp.exp2(rg * (softplus * (-LRU_C * LOG2E)))
        gap = 1.0 - a * a
        root = jnp.where(gap > 0.0, gap * lax.rsqrt(gap), 0.0)
        uu = root * (ig * xc)
        step = 1
        while step < TB:
            uu = a * _shift_rows(uu, step, 0.0) + uu
            a = a * _shift_rows(a, step, 1.0)
            step *= 2
        hs = uu + a * hcar[0:1, :]
        hcar[0:1, :] = hs[TB - 1:TB, :]
        shared["hs"] = hs
        ycat[:, 2 * W:3 * W] = (hs * gates[:, 2 * W:3 * W]).astype(BF16)

    def mixer_d():
        dq = pd[:, 0:LANES] * (D_KEY ** -0.5)
        dk = pd[:, LANES:2 * LANES]
        dv = pd[:, 2 * LANES:2 * LANES + W]
        z = _dot(plr[...].astype(BF16), dwup[...]) + prow(dbu_ref)
        yield
        lg = (jnp.minimum(z, 0.0) - jnp.log1p(jnp.exp2(jnp.abs(z) * -LOG2E))) * (LOG2E / GATE_TAU)
        lg_hi = lg.astype(BF16)
        lg_lo = (lg - lg_hi.astype(F32)).astype(BF16)
        pieces = jnp.concatenate([lg_hi, lg_lo], axis=1)
        both = jnp.concatenate([_dot(tri[...], pieces[r0:r0 + CUM_ROWS]) for r0 in range(0, TB, CUM_ROWS)], axis=0)
        bcum = both[:, 0:LANES] + both[:, LANES:2 * LANES]
        yield
        lane = lax.broadcasted_iota(jnp.int32, (1, LANES), 1)
        lane256 = lax.broadcasted_iota(jnp.int32, (1, W), 1)
        ci = lax.broadcasted_iota(jnp.int32, (CHUNK, W), 0)
        cj = lax.broadcasted_iota(jnp.int32, (CHUNK, W), 1) % CHUNK
        causal = (ci >= cj).astype(F32)
        bdm = (lax.broadcasted_iota(jnp.int32, (W, LANES), 0) // HEAD
               == lax.broadcasted_iota(jnp.int32, (W, LANES), 1) // D_KEY).astype(F32)
        state = st[...]
        od_parts = []
        for c in range(TB // CHUNK):
            r0 = c * CHUNK
            bc = bcum[r0:r0 + CHUNK, :]
            bl = bc[CHUNK - 1:CHUNK, :]
            qt = (dq[r0:r0 + CHUNK, :] * jnp.exp2(bc)).astype(BF16)
            kc = dk[r0:r0 + CHUNK, :]
            kt = kc * jnp.exp2(-bc)
            ke = (kc * jnp.exp2(bl - bc)).astype(BF16)
            vc = dv[r0:r0 + CHUNK, :]
            kst = jnp.concatenate([kt * (lane // D_KEY == h).astype(F32) for h in range(D_HEADS)],
                                  axis=0).astype(BF16)
            att = (_dot_nt(qt, kst) * causal).astype(BF16)
            yield
            vbd = jnp.concatenate([vc * (lane256 // HEAD == h).astype(F32) for h in range(D_HEADS)],
                                  axis=0).astype(BF16)
            o_c = _dot(att, vbd) + _dot_nt(qt, state.astype(BF16))
            state = state * jnp.exp2(bl) + _dot_tn(vc.astype(BF16), ke) * bdm
            od_parts.append(o_c)
            yield
        st[...] = state
        od = jnp.concatenate(od_parts, axis=0)
        odn = od * lax.rsqrt(_dot_split(od * od, gmat, 1) + EPS) * prow(dng_ref)
        ycat[:, 3 * W:4 * W] = (odn * gates[:, 3 * W:4 * W]).astype(BF16)

    def out_proj_prev():
        for c0 in range(0, D_MODEL, OUT_SLAB):
            o_ref[0, :, c0:c0 + OUT_SLAB] = (xo_ref[0, :, c0:c0 + OUT_SLAB]
                                             + _dot(yprev[...], wout_ref[:, c0:c0 + OUT_SLAB]))
            yield

    @pl.when(s_idx < n_tiles)
    def main_step():
        nonlocal hb, gmat
        x = x_ref[0]
        ms = jnp.mean(x * x, axis=-1, keepdims=True)
        hb = (x * lax.rsqrt(ms + EPS) * prow(ng_ref)).astype(BF16)
        gmat = g256[...]
        gens = {"A": mixer_a(), "B": mixer_b(), "C": mixer_c(), "D": mixer_d(), "O": out_proj_prev(),
                "P": project()}
        for name in TRACE_ORDER:
            next(gens[name], None)
        running = [gens[k] for k in "PDABCO"]
        while running:
            running = [g for g in running if next(g, StopIteration) is not StopIteration]
        yprev[...] = ycat[...]

    @pl.when(s_idx == n_tiles)
    def last_step():
        for _ in out_proj_prev():
            pass


def _t5_bucket(dist):
    max_exact = N_BUCKETS // 2
    d = jnp.maximum(dist, 1).astype(F32)
    large = max_exact + (jnp.log(d / max_exact) / math.log(MAX_DISTANCE / max_exact)
                         * (N_BUCKETS - max_exact)).astype(jnp.int32)
    large = jnp.minimum(large, N_BUCKETS - 1)
    return jnp.where(dist < max_exact, dist, large)


def _layer_spec(layer, shape):
    return pl.BlockSpec((None,) + shape, lambda s: (layer,) + (0,) * len(shape))


def _whole(shape):
    return pl.BlockSpec(shape, lambda s: (0,) * len(shape))


def _layer_call(layer, tiles_per_seq, x, ng, win, wtail, wout, apw, cwr, cwi, dwup, aconv, vec256, ccw, dbu,
                dng, bqg, bkg, bucket, sinks, relb):
    n_tiles = x.shape[0]
    smem = pl.BlockSpec(memory_space=pltpu.SMEM)
    spec = functools.partial(_layer_spec, layer)
    acb, alg, alb, ccb, cbr, cbi, lam = vec256
    prev_tile = lambda s: (jnp.maximum(s - 1, 0), 0, 0)
    return pl.pallas_call(
        functools.partial(_layer_kernel, layer, tiles_per_seq, n_tiles),
        grid=(n_tiles + 1,),
        in_specs=[
            pl.BlockSpec((1, TB, D_MODEL), lambda s: (jnp.minimum(s, n_tiles - 1), 0, 0)),
            pl.BlockSpec((1, TB, D_MODEL), prev_tile),
            _whole((DEPTH, D_MODEL)), spec((D_MODEL, IN_COLS)), spec((D_MODEL, D_TAIL)),
            spec((D_MODEL, D_MODEL)), spec((W, W)), spec((W, W)), spec((W, W)),
            spec((GATE_RANK, LANES)), spec((CONV_A, W)),
            _whole((DEPTH, W)), _whole((DEPTH, W)), _whole((DEPTH, W)),
            spec((CONV_C, W)),
            _whole((DEPTH, W)), _whole((DEPTH, W)), _whole((DEPTH, W)), _whole((DEPTH, W)),
            _whole((DEPTH, LANES)),
            _whole((DEPTH, W)), _whole((DEPTH, W)), _whole((DEPTH, LANES)),
            _whole((QBLK, 2 * QBLK)),
            smem, smem,
        ],
        out_specs=pl.BlockSpec((1, TB, D_MODEL), prev_tile),
        out_shape=jax.ShapeDtypeStruct(x.shape, F32),
        scratch_shapes=[
            pltpu.VMEM((TB + A_PAD, W), F32),
            pltpu.VMEM((TB + C_PAD, W), F32),
            pltpu.VMEM((SUBLANES, W), F32),
            pltpu.VMEM((TB + QBLK, LANES), BF16),
            pltpu.VMEM((TB + QBLK, LANES), BF16),
            pltpu.VMEM((W, LANES), F32),
            pltpu.VMEM((TB, D_MODEL), BF16),
            pltpu.VMEM((4 * QBLK, 2 * QBLK), F32),
            pltpu.VMEM((W, W), BF16),
            pltpu.VMEM((CUM_ROWS, CUM_ROWS), BF16),
            pltpu.VMEM((LANES, LANES), BF16),
            pltpu.VMEM((TB, D_MODEL), F32),
            pltpu.VMEM((TB, B_QKV), F32),
            pltpu.VMEM((TB, D_QKV), F32),
            pltpu.VMEM((TB, LANES), F32),
            pltpu.VMEM((TB, D_MODEL), BF16),
        ],
        compiler_params=pltpu.CompilerParams(
            dimension_semantics=("arbitrary",),
            vmem_limit_bytes=VMEM_LIMIT_BYTES,
        ),
        name="hybrid_layer",
    )(x, x, ng, win, wtail, wout, apw, cwr, cwi, dwup, aconv, acb, alg, alb, ccw, ccb, cbr, cbi, lam, dbu,
      dng, bqg, bkg, bucket, sinks, relb)


def _block_diag(blocks):
    depth, n, c, _ = blocks.shape
    eye = jnp.eye(n, dtype=blocks.dtype)
    return (eye[None, :, None, :, None] * blocks[:, :, :, None, :]).reshape(depth, n * c, n * c)


def kernel(x, norm_g, w_in, a_conv_w, a_conv_b, a_ln_g, a_ln_b, a_pw, b_q_g, b_k_g, b_sinks, rel_bias,
           c_conv_w, c_conv_b, c_w_r, c_b_r, c_w_i, c_b_i, c_lambda, d_w_up, d_b_up, d_norm_g, w_out):
    dist = jnp.arange(QBLK)[:, None] + QBLK - jnp.arange(2 * QBLK)[None, :]
    bucket = _t5_bucket(jnp.clip(dist, 0, None)).astype(jnp.int32)
    wtail = jnp.concatenate([
        w_in[:, :, D_TAIL_SRC + GATE_RANK:IN_COLS], w_in[:, :, D_TAIL_SRC:D_TAIL_SRC + GATE_RANK],
        jnp.zeros((DEPTH, D_MODEL, D_TAIL - (IN_COLS - D_TAIL_SRC)), F32)], axis=2).astype(BF16)
    vec256 = (a_conv_b, a_ln_g, a_ln_b, c_conv_b, c_b_r, c_b_i, c_lambda)
    args = (norm_g, w_in.astype(BF16), wtail, w_out.astype(BF16), a_pw.astype(BF16),
            _block_diag(c_w_r).astype(BF16), _block_diag(c_w_i).astype(BF16), d_w_up, a_conv_w,
            vec256, c_conv_w, d_b_up, jnp.tile(d_norm_g, (1, 4)), jnp.tile(b_q_g, (1, 4)),
            jnp.tile(b_k_g, (1, 2)), bucket, b_sinks.reshape(-1), rel_bias.reshape(-1))
    bsz, seq, _ = x.shape
    assert seq % TB == 0
    h = x.reshape(bsz * seq // TB, TB, D_MODEL)
    for l in range(DEPTH):
        h = _layer_call(l, seq // TB, h, *args)
    return h.reshape(bsz, seq, D_MODEL)
```

```python
import functools
import math

import jax
import jax.numpy as jnp
from jax import lax
from jax.experimental import pallas as pl
from jax.experimental.pallas import tpu as pltpu

F32 = jnp.float32
BF16 = jnp.bfloat16

D_MODEL = 1024
DEPTH = 4
W = 256
EPS = 1e-6
CONV_A = 31
A_PAD = 32
HEAD = 64
WINDOW = 128
QBLK = 128
N_BUCKETS = 32
MAX_DISTANCE = 128
CONV_C = 4
C_PAD = 8
LRU_C = 8.0
D_HEADS = 4
D_KEY = 32
GATE_RANK = 16
GATE_TAU = 16.0
CHUNK = 64
SUBLANES = 8
LANES = 128
VMEM_LIMIT_BYTES = 48 * 1024 * 1024
LOG2E = math.log2(math.e)

TB = 512
TRACE_ORDER = ("O P A P P C P P O B D P C O D B P D D B D D O P D D D D B D D D D B D D D D B P A B B B A A "
               "D B").split()
OUT_SLAB = 256
CUM_ROWS = 256

A_OFF = 0
B_OFF = 768
B_QKV = 512
C_OFF = 1536
D_OFF, D_QKV = 2048, 512
D_TAIL_SRC = D_OFF + D_QKV
D_TAIL = 384
IN_COLS = 2832


def _dot(a, b):
    return jnp.dot(a, b, preferred_element_type=F32)


def _dot_nt(a, b):
    return lax.dot_general(a, b, (((1,), (1,)), ((), ())), preferred_element_type=F32)


def _dot_tn(a, b):
    return lax.dot_general(a, b, (((0,), (0,)), ((), ())), preferred_element_type=F32)


def _dot_split(x, w_bf, passes):
    acc = None
    r = x
    for p in range(passes):
        part = r.astype(BF16)
        d = _dot(part, w_bf)
        acc = d if acc is None else acc + d
        if p + 1 < passes:
            r = r - part.astype(F32)
    return acc


def _sigmoid(x):
    return 1.0 / (1.0 + jnp.exp2(x * -LOG2E))


def _silu(x):
    return x * _sigmoid(x)


def _causal_taps(acc, buf, w_ref, n_taps, pad):
    rows = buf.shape[0]
    for res in range(SUBLANES):
        shifted = None
        for j in range(n_taps):
            off = pad - (n_taps - 1) + j
            if off % SUBLANES != res:
                continue
            if shifted is None:
                shifted = buf if res == 0 else pltpu.roll(buf, rows - res, axis=0)
            base = off - res
            acc = acc + w_ref[j:j + 1, :] * shifted[base:base + TB]
    return acc


def _shift_rows(x, step, fill):
    if step % SUBLANES == 0:
        return jnp.concatenate([jnp.full((step, x.shape[1]), fill, F32), x[0:x.shape[0] - step]], axis=0)
    rolled = pltpu.roll(x, step, axis=0)
    row = lax.broadcasted_iota(jnp.int32, (SUBLANES, x.shape[1]), 0)
    head = jnp.where(row < step, fill, rolled[0:SUBLANES])
    return jnp.concatenate([head, rolled[SUBLANES:]], axis=0)


def _layer_kernel(layer, tiles_per_seq, n_tiles,
                  x_ref, xo_ref, ng_ref, win_ref, wtail_ref, wout_ref, apw_ref, cwr_ref, cwi_ref, dwup_ref, aconv_ref,
                  acb_ref, alg_ref, alb_ref, ccw_ref, ccb_ref, cbr_ref, cbi_ref, lam_ref, dbu_ref,
                  dng_ref, bqg_ref, bkg_ref, bucket_ref, sinks_ref, relb_ref,
                  o_ref,
                  ubuf, cbuf, hcar, kbuf, vbuf, st, ycat, biasm, g256, tri, dwup, gates, pb, pd, plr, yprev):
    s_idx = pl.program_id(0)
    seq_start = (s_idx % tiles_per_seq) == 0
    prow = lambda ref: ref[layer:layer + 1, :]

    @pl.when(s_idx == 0)
    def _build_tables():
        yprev[...] = jnp.zeros((TB, D_MODEL), BF16)
        ri = lax.broadcasted_iota(jnp.int32, (W, W), 0) // HEAD
        ci = lax.broadcasted_iota(jnp.int32, (W, W), 1) // HEAD
        g256[...] = jnp.where(ri == ci, 1.0 / HEAD, 0.0).astype(BF16)
        rt = lax.broadcasted_iota(jnp.int32, (CUM_ROWS, CUM_ROWS), 0)
        ct = lax.broadcasted_iota(jnp.int32, (CUM_ROWS, CUM_ROWS), 1)
        tri[...] = jnp.where((rt // CHUNK == ct // CHUNK) & (rt >= ct), 1.0, 0.0).astype(BF16)
        dwup[...] = jnp.zeros((LANES, LANES), BF16)
        dwup[0:GATE_RANK, :] = dwup_ref[...].astype(BF16)
        bucket = bucket_ref[...]
        qi = lax.broadcasted_iota(jnp.int32, (QBLK, 2 * QBLK), 0)
        kj = lax.broadcasted_iota(jnp.int32, (QBLK, 2 * QBLK), 1)
        dist = qi + QBLK - kj
        valid = (dist >= 0) & (dist < WINDOW)
        for h in range(4):
            acc = jnp.zeros((QBLK, 2 * QBLK), F32)
            for bk in range(N_BUCKETS):
                acc = jnp.where(bucket == bk, relb_ref[bk * 4 + h], acc)
            biasm[h * QBLK:(h + 1) * QBLK, :] = jnp.where(valid, acc * LOG2E, -jnp.inf)

    @pl.when(seq_start)
    def _reset_state():
        ubuf[0:A_PAD, :] = jnp.zeros((A_PAD, W), F32)
        cbuf[0:C_PAD, :] = jnp.zeros((C_PAD, W), F32)
        hcar[...] = jnp.zeros((SUBLANES, W), F32)
        kbuf[0:QBLK, :] = jnp.zeros((QBLK, LANES), BF16)
        vbuf[0:QBLK, :] = jnp.zeros((QBLK, LANES), BF16)
        st[...] = jnp.zeros((W, LANES), F32)

    hb = gmat = None

    proj = lambda off, width: _dot(hb, win_ref[:, off:off + width])
    shared = {}

    def project():
        pav = proj(A_OFF, 2 * W)
        ubuf[A_PAD:A_PAD + TB, :] = pav[:, 0:W] * _sigmoid(pav[:, W:2 * W])
        yield
        cbuf[C_PAD:C_PAD + TB, :] = proj(C_OFF, W)
        yield
        pb[...] = proj(B_OFF, B_QKV)
        yield
        pd[...] = proj(D_OFF, D_QKV)
        yield
        plr[...] = _dot(hb, wtail_ref[:, W:W + LANES])
        yield
        gates[:, 2 * W:3 * W] = _silu(proj(C_OFF + W, W))
        yield
        gates[:, 0:W] = _silu(proj(A_OFF + 2 * W, W))
        yield
        gates[:, W:2 * W] = _silu(proj(B_OFF + B_QKV, W))
        yield
        gates[:, 3 * W:4 * W] = _silu(_dot(hb, wtail_ref[:, 0:W]))


    def mixer_a():
        conv = jnp.broadcast_to(prow(acb_ref), (TB, W))
        conv = _causal_taps(conv, ubuf[...], aconv_ref, CONV_A, A_PAD)
        ubuf[0:A_PAD, :] = ubuf[TB:TB + A_PAD, :]
        yield
        mu = _dot_split(conv, gmat, 2)
        yield
        dc = conv - mu
        var = _dot_split(dc * dc, gmat, 1)
        yield
        un = dc * lax.rsqrt(var + EPS) * prow(alg_ref) + prow(alb_ref)
        ya = _dot(_silu(un).astype(BF16), apw_ref[...]) * gates[:, 0:W]
        ycat[:, 0:W] = ya.astype(BF16)

    def mixer_b():
        q = pb[:, 0:W]
        k = pb[:, W:W + LANES]
        v = pb[:, W + LANES:W + 2 * LANES]
        qn = q * lax.rsqrt(_dot_split(q * q, gmat, 1) + EPS)
        kn = k * lax.rsqrt(_dot_split(k * k, gmat[0:LANES, 0:LANES], 1) + EPS)
        yield
        kbuf[QBLK:QBLK + TB, :] = kn.astype(BF16)
        vbuf[QBLK:QBLK + TB, :] = v.astype(BF16)
        lane = lax.broadcasted_iota(jnp.int32, (1, LANES), 1)
        lanem = lane < HEAD
        qk_gain = prow(bqg_ref)[:, 0:LANES] * prow(bkg_ref) * (HEAD ** -0.5 * LOG2E)
        hs_bits = pltpu.bitcast(shared["hs"][TB - SUBLANES:TB, LANES:2 * LANES], jnp.int32)
        zero = (hs_bits & jnp.minimum(s_idx, 0))[0:1, :]
        qk_gain = pltpu.bitcast(pltpu.bitcast(qk_gain, jnp.int32) | zero, F32)
        lo = jnp.where(lanem, qk_gain, 0.0)
        hi = jnp.where(lanem, 0.0, qk_gain)
        first_tile_mask = jnp.where(seq_start, -jnp.inf, 0.0)

        def softmax_rows(sh, h, bi):
            if bi == 0:
                sh = jnp.concatenate([sh[:, 0:QBLK] + first_tile_mask, sh[:, QBLK:]], axis=1)
            sink = sinks_ref[layer * 4 + h] * LOG2E
            m = jnp.maximum(jnp.max(sh, axis=-1, keepdims=True), sink)
            p = jnp.exp2(sh - m)
            den = jnp.sum(p, axis=-1, keepdims=True) + jnp.exp2(sink - m)
            return p.astype(BF16), den

        for bi in range(TB // QBLK):
            r0 = bi * QBLK
            qa = qn[r0:r0 + QBLK, 0:LANES]
            qb = qn[r0:r0 + QBLK, LANES:2 * LANES]
            qst = jnp.concatenate([qa * lo, pltpu.roll(qa, HEAD, axis=1) * lo,
                                   pltpu.roll(qb, HEAD, axis=1) * hi, qb * hi], axis=0).astype(BF16)
            s = _dot_nt(qst, kbuf[r0:r0 + 2 * QBLK, :]) + biasm[...]
            yield
            pden = [softmax_rows(s[h * QBLK:(h + 1) * QBLK, :], h, bi) for h in range(4)]
            ov = _dot(jnp.concatenate([t[0] for t in pden], axis=0), vbuf[r0:r0 + 2 * QBLK, :])
            yield
            oh = [ov[h * QBLK:(h + 1) * QBLK, :] / pden[h][1] for h in range(4)]
            y01 = jnp.where(lanem, oh[0], pltpu.roll(oh[1], HEAD, axis=1))
            y23 = jnp.where(lanem, pltpu.roll(oh[2], HEAD, axis=1), oh[3])
            ycat[r0:r0 + QBLK, W:2 * W] = (jnp.concatenate([y01, y23], axis=1)
                                           * gates[r0:r0 + QBLK, W:2 * W]).astype(BF16)
        kbuf[0:QBLK, :] = kbuf[TB:TB + QBLK, :]
        vbuf[0:QBLK, :] = vbuf[TB:TB + QBLK, :]

    def mixer_c():
        xc = jnp.broadcast_to(prow(ccb_ref), (TB, W))
        xc = _causal_taps(xc, cbuf[...], ccw_ref, CONV_C, C_PAD)
        cbuf[0:C_PAD, :] = cbuf[TB:TB + C_PAD, :]
        xcb = xc.astype(BF16)
        rg = _sigmoid(_dot(xcb, cwr_ref[...]) + prow(cbr_ref))
        ig = _sigmoid(_dot(xcb, cwi_ref[...]) + prow(cbi_ref))
        yield
        nlam = -prow(lam_ref)
        softplus = jnp.maximum(nlam, 0.0) + jnp.log1p(jnp.exp(-jnp.abs(nlam)))
        a = jnp.exp2(rg * (softplus * (-LRU_C * LOG2E)))
        gap = 1.0 - a * a
        root = jnp.where(gap > 0.0, gap * lax.rsqrt(gap), 0.0)
        uu = root * (ig * xc)
        groups = TB // SUBLANES
        a3 = a.reshape(groups, SUBLANES, W)
        u3 = uu.reshape(groups, SUBLANES, W)
        row = lax.broadcasted_iota(jnp.int32, (1, SUBLANES, W), 1)
        step = 1
        while step < SUBLANES:
            inside = row >= step
            u3 = a3 * jnp.where(inside, pltpu.roll(u3, step, axis=1), 0.0) + u3
            a3 = a3 * jnp.where(inside, pltpu.roll(a3, step, axis=1), 1.0)
            step *= 2
        h_in = hcar[0:1, :]
        h_groups = []
        for g in range(groups):
            h_g = u3[g] + a3[g] * h_in
            h_groups.append(h_g)
            h_in = h_g[SUBLANES - 1:SUBLANES, :]
        hs = jnp.concatenate(h_groups, axis=0)
        hcar[0:1, :] = h_in
        shared["hs"] = hs
        ycat[:, 2 * W:3 * W] = (hs * gates[:, 2 * W:3 * W]).astype(BF16)

    def mixer_d():
        dq = pd[:, 0:LANES] * (D_KEY ** -0.5)
        dk = pd[:, LANES:2 * LANES]
        dv = pd[:, 2 * LANES:2 * LANES + W]
        z = _dot(plr[...].astype(BF16), dwup[...]) + prow(dbu_ref)
        yield
        lg = (jnp.minimum(z, 0.0) - jnp.log1p(jnp.exp2(jnp.abs(z) * -LOG2E))) * (LOG2E / GATE_TAU)
        lg_hi = lg.astype(BF16)
        lg_lo = (lg - lg_hi.astype(F32)).astype(BF16)
        pieces = jnp.concatenate([lg_hi, lg_lo], axis=1)
        both = jnp.concatenate([_dot(tri[...], pieces[r0:r0 + CUM_ROWS]) for r0 in range(0, TB, CUM_ROWS)], axis=0)
        bcum = both[:, 0:LANES] + both[:, LANES:2 * LANES]
        yield
        lane = lax.broadcasted_iota(jnp.int32, (1, LANES), 1)
        lane256 = lax.broadcasted_iota(jnp.int32, (1, W), 1)
        ci = lax.broadcasted_iota(jnp.int32, (CHUNK, W), 0)
        cj = lax.broadcasted_iota(jnp.int32, (CHUNK, W), 1) % CHUNK
        causal = (ci >= cj).astype(F32)
        bdm = (lax.broadcasted_iota(jnp.int32, (W, LANES), 0) // HEAD
               == lax.broadcasted_iota(jnp.int32, (W, LANES), 1) // D_KEY).astype(F32)
        state = st[...]
        od_parts = []
        for c in range(TB // CHUNK):
            r0 = c * CHUNK
            bc = bcum[r0:r0 + CHUNK, :]
            bl = bc[CHUNK - 1:CHUNK, :]
            qt = (dq[r0:r0 + CHUNK, :] * jnp.exp2(bc)).astype(BF16)
            kc = dk[r0:r0 + CHUNK, :]
            kt = kc * jnp.exp2(-bc)
            ke = (kc * jnp.exp2(bl - bc)).astype(BF16)
            vc = dv[r0:r0 + CHUNK, :]
            kst = jnp.concatenate([kt * (lane // D_KEY == h).astype(F32) for h in range(D_HEADS)],
                                  axis=0).astype(BF16)
            att = (_dot_nt(qt, kst) * causal).astype(BF16)
            yield
            vbd = jnp.concatenate([vc * (lane256 // HEAD == h).astype(F32) for h in range(D_HEADS)],
                                  axis=0).astype(BF16)
            o_c = _dot(att, vbd) + _dot_nt(qt, state.astype(BF16))
            state = state * jnp.exp2(bl) + _dot_tn(vc.astype(BF16), ke) * bdm
            od_parts.append(o_c)
            yield
        st[...] = state
        od = jnp.concatenate(od_parts, axis=0)
        odn = od * lax.rsqrt(_dot_split(od * od, gmat, 1) + EPS) * prow(dng_ref)
        ycat[:, 3 * W:4 * W] = (odn * gates[:, 3 * W:4 * W]).astype(BF16)

    def out_proj_prev():
        for c0 in range(0, D_MODEL, OUT_SLAB):
            o_ref[0, :, c0:c0 + OUT_SLAB] = (xo_ref[0, :, c0:c0 + OUT_SLAB]
                                             + _dot(yprev[...], wout_ref[:, c0:c0 + OUT_SLAB]))
            yield

    @pl.when(s_idx < n_tiles)
    def main_step():
        nonlocal hb, gmat
        x = x_ref[0]
        ms = jnp.mean(x * x, axis=-1, keepdims=True)
        hb = (x * lax.rsqrt(ms + EPS) * prow(ng_ref)).astype(BF16)
        gmat = g256[...]
        gens = {"A": mixer_a(), "B": mixer_b(), "C": mixer_c(), "D": mixer_d(), "O": out_proj_prev(),
                "P": project()}
        for name in TRACE_ORDER:
            next(gens[name], None)
        running = [gens[k] for k in "PDABCO"]
        while running:
            running = [g for g in running if next(g, StopIteration) is not StopIteration]
        yprev[...] = ycat[...]

    @pl.when(s_idx == n_tiles)
    def last_step():
        for _ in out_proj_prev():
            pass


def _t5_bucket(dist):
    max_exact = N_BUCKETS // 2
    d = jnp.maximum(dist, 1).astype(F32)
    large = max_exact + (jnp.log(d / max_exact) / math.log(MAX_DISTANCE / max_exact)
                         * (N_BUCKETS - max_exact)).astype(jnp.int32)
    large = jnp.minimum(large, N_BUCKETS - 1)
    return jnp.where(dist < max_exact, dist, large)


def _layer_spec(layer, shape):
    return pl.BlockSpec((None,) + shape, lambda s: (layer,) + (0,) * len(shape))


def _whole(shape):
    return pl.BlockSpec(shape, lambda s: (0,) * len(shape))


def _layer_call(layer, tiles_per_seq, x, ng, win, wtail, wout, apw, cwr, cwi, dwup, aconv, vec256, ccw, dbu,
                dng, bqg, bkg, bucket, sinks, relb):
    n_tiles = x.shape[0]
    smem = pl.BlockSpec(memory_space=pltpu.SMEM)
    spec = functools.partial(_layer_spec, layer)
    acb, alg, alb, ccb, cbr, cbi, lam = vec256
    prev_tile = lambda s: (jnp.maximum(s - 1, 0), 0, 0)
    return pl.pallas_call(
        functools.partial(_layer_kernel, layer, tiles_per_seq, n_tiles),
        grid=(n_tiles + 1,),
        in_specs=[
            pl.BlockSpec((1, TB, D_MODEL), lambda s: (jnp.minimum(s, n_tiles - 1), 0, 0)),
            pl.BlockSpec((1, TB, D_MODEL), prev_tile),
            _whole((DEPTH, D_MODEL)), spec((D_MODEL, IN_COLS)), spec((D_MODEL, D_TAIL)),
            spec((D_MODEL, D_MODEL)), spec((W, W)), spec((W, W)), spec((W, W)),
            spec((GATE_RANK, LANES)), spec((CONV_A, W)),
            _whole((DEPTH, W)), _whole((DEPTH, W)), _whole((DEPTH, W)),
            spec((CONV_C, W)),
            _whole((DEPTH, W)), _whole((DEPTH, W)), _whole((DEPTH, W)), _whole((DEPTH, W)),
            _whole((DEPTH, LANES)),
            _whole((DEPTH, W)), _whole((DEPTH, W)), _whole((DEPTH, LANES)),
            _whole((QBLK, 2 * QBLK)),
            smem, smem,
        ],
        out_specs=pl.BlockSpec((1, TB, D_MODEL), prev_tile),
        out_shape=jax.ShapeDtypeStruct(x.shape, F32),
        scratch_shapes=[
            pltpu.VMEM((TB + A_PAD, W), F32),
            pltpu.VMEM((TB + C_PAD, W), F32),
            pltpu.VMEM((SUBLANES, W), F32),
            pltpu.VMEM((TB + QBLK, LANES), BF16),
            pltpu.VMEM((TB + QBLK, LANES), BF16),
            pltpu.VMEM((W, LANES), F32),
            pltpu.VMEM((TB, D_MODEL), BF16),
            pltpu.VMEM((4 * QBLK, 2 * QBLK), F32),
            pltpu.VMEM((W, W), BF16),
            pltpu.VMEM((CUM_ROWS, CUM_ROWS), BF16),
            pltpu.VMEM((LANES, LANES), BF16),
            pltpu.VMEM((TB, D_MODEL), F32),
            pltpu.VMEM((TB, B_QKV), F32),
            pltpu.VMEM((TB, D_QKV), F32),
            pltpu.VMEM((TB, LANES), F32),
            pltpu.VMEM((TB, D_MODEL), BF16),
        ],
        compiler_params=pltpu.CompilerParams(
            dimension_semantics=("arbitrary",),
            vmem_limit_bytes=VMEM_LIMIT_BYTES,
        ),
        name="hybrid_layer",
    )(x, x, ng, win, wtail, wout, apw, cwr, cwi, dwup, aconv, acb, alg, alb, ccw, ccb, cbr, cbi, lam, dbu,
      dng, bqg, bkg, bucket, sinks, relb)


def _block_diag(blocks):
    depth, n, c, _ = blocks.shape
    eye = jnp.eye(n, dtype=blocks.dtype)
    return (eye[None, :, None, :, None] * blocks[:, :, :, None, :]).reshape(depth, n * c, n * c)


def kernel(x, norm_g, w_in, a_conv_w, a_conv_b, a_ln_g, a_ln_b, a_pw, b_q_g, b_k_g, b_sinks, rel_bias,
           c_conv_w, c_conv_b, c_w_r, c_b_r, c_w_i, c_b_i, c_lambda, d_w_up, d_b_up, d_norm_g, w_out):
    dist = jnp.arange(QBLK)[:, None] + QBLK - jnp.arange(2 * QBLK)[None, :]
    bucket = _t5_bucket(jnp.clip(dist, 0, None)).astype(jnp.int32)
    wtail = jnp.concatenate([
        w_in[:, :, D_TAIL_SRC + GATE_RANK:IN_COLS], w_in[:, :, D_TAIL_SRC:D_TAIL_SRC + GATE_RANK],
        jnp.zeros((DEPTH, D_MODEL, D_TAIL - (IN_COLS - D_TAIL_SRC)), F32)], axis=2).astype(BF16)
    vec256 = (a_conv_b, a_ln_g, a_ln_b, c_conv_b, c_b_r, c_b_i, c_lambda)
    args = (norm_g, w_in.astype(BF16), wtail, w_out.astype(BF16), a_pw.astype(BF16),
            _block_diag(c_w_r).astype(BF16), _block_diag(c_w_i).astype(BF16), d_w_up, a_conv_w,
            vec256, c_conv_w, d_b_up, jnp.tile(d_norm_g, (1, 4)), jnp.tile(b_q_g, (1, 4)),
            jnp.tile(b_k_g, (1, 2)), bucket, b_sinks.reshape(-1), rel_bias.reshape(-1))
    bsz, seq, _ = x.shape
    assert seq % TB == 0
    h = x.reshape(bsz * seq // TB, TB, D_MODEL)
    for l in range(DEPTH):
        h = _layer_call(l, seq // TB, h, *args)
    return h.reshape(bsz, seq, D_MODEL)
```

```python
import functools
import math

import jax
import jax.numpy as jnp
from jax import lax
from jax.experimental import pallas as pl
from jax.experimental.pallas import tpu as pltpu

F32 = jnp.float32
BF16 = jnp.bfloat16

D_MODEL = 1024
DEPTH = 4
W = 256
EPS = 1e-6
CONV_A = 31
A_PAD = 32
HEAD = 64
WINDOW = 128
QBLK = 128
N_BUCKETS = 32
MAX_DISTANCE = 128
CONV_C = 4
C_PAD = 8
LRU_C = 8.0
D_HEADS = 4
D_KEY = 32
GATE_RANK = 16
GATE_TAU = 16.0
CHUNK = 64
SUBLANES = 8
LANES = 128
VMEM_LIMIT_BYTES = 48 * 1024 * 1024
LOG2E = math.log2(math.e)

TB = 512
TRACE_ORDER = ("O P A P P C P P O B D P C O D B P D D B D D O P D D D D B D D D D B D D D D B P A B B B A A "
               "D B").split()
OUT_SLAB = 256
CUM_ROWS = 256

A_OFF = 0
B_OFF = 768
B_QKV = 512
C_OFF = 1536
D_OFF, D_QKV = 2048, 512
D_TAIL_SRC = D_OFF + D_QKV
D_TAIL = 384
IN_COLS = 2832


def _dot(a, b):
    return jnp.dot(a, b, preferred_element_type=F32)


def _dot_nt(a, b):
    return lax.dot_general(a, b, (((1,), (1,)), ((), ())), preferred_element_type=F32)


def _dot_tn(a, b):
    return lax.dot_general(a, b, (((0,), (0,)), ((), ())), preferred_element_type=F32)


def _dot_split(x, w_bf, passes):
    acc = None
    r = x
    for p in range(passes):
        part = r.astype(BF16)
        d = _dot(part, w_bf)
        acc = d if acc is None else acc + d
        if p + 1 < passes:
            r = r - part.astype(F32)
    return acc


def _sigmoid(x):
    return 1.0 / (1.0 + jnp.exp2(x * -LOG2E))


def _silu(x):
    return x * _sigmoid(x)


def _causal_taps(acc, buf, w_ref, n_taps, pad):
    rows = buf.shape[0]
    for res in range(SUBLANES):
        shifted = None
        for j in range(n_taps):
            off = pad - (n_taps - 1) + j
            if off % SUBLANES != res:
                continue
            if shifted is None:
                shifted = buf if res == 0 else pltpu.roll(buf, rows - res, axis=0)
            base = off - res
            acc = acc + w_ref[j:j + 1, :] * shifted[base:base + TB]
    return acc


def _shift_rows(x, step, fill):
    if step % SUBLANES == 0:
        return jnp.concatenate([jnp.full((step, x.shape[1]), fill, F32), x[0:x.shape[0] - step]], axis=0)
    rolled = pltpu.roll(x, step, axis=0)
    row = lax.broadcasted_iota(jnp.int32, (SUBLANES, x.shape[1]), 0)
    head = jnp.where(row < step, fill, rolled[0:SUBLANES])
    return jnp.concatenate([head, rolled[SUBLANES:]], axis=0)


def _layer_kernel(layer, tiles_per_seq, n_tiles,
                  x_ref, xo_ref, ng_ref, win_ref, wtail_ref, wout_ref, apw_ref, cwr_ref, cwi_ref, dwup_ref, aconv_ref,
                  acb_ref, alg_ref, alb_ref, ccw_ref, ccb_ref, cbr_ref, cbi_ref, lam_ref, dbu_ref,
                  dng_ref, bqg_ref, bkg_ref, bucket_ref, sinks_ref, relb_ref,
                  o_ref,
                  ubuf, cbuf, hcar, kbuf, vbuf, st, ycat, biasm, g256, tri, dwup, gates, pb, pd, plr, yprev):
    s_idx = pl.program_id(0)
    seq_start = (s_idx % tiles_per_seq) == 0
    prow = lambda ref: ref[layer:layer + 1, :]

    @pl.when(s_idx == 0)
    def _build_tables():
        yprev[...] = jnp.zeros((TB, D_MODEL), BF16)
        ri = lax.broadcasted_iota(jnp.int32, (W, W), 0) // HEAD
        ci = lax.broadcasted_iota(jnp.int32, (W, W), 1) // HEAD
        g256[...] = jnp.where(ri == ci, 1.0 / HEAD, 0.0).astype(BF16)
        rt = lax.broadcasted_iota(jnp.int32, (CUM_ROWS, CUM_ROWS), 0)
        ct = lax.broadcasted_iota(jnp.int32, (CUM_ROWS, CUM_ROWS), 1)
        tri[...] = jnp.where((rt // CHUNK == ct // CHUNK) & (rt >= ct), 1.0, 0.0).astype(BF16)
        dwup[...] = jnp.zeros((LANES, LANES), BF16)
        dwup[0:GATE_RANK, :] = dwup_ref[...].astype(BF16)
        bucket = bucket_ref[...]
        qi = lax.broadcasted_iota(jnp.int32, (QBLK, 2 * QBLK), 0)
        kj = lax.broadcasted_iota(jnp.int32, (QBLK, 2 * QBLK), 1)
        dist = qi + QBLK - kj
        valid = (dist >= 0) & (dist < WINDOW)
        for h in range(4):
            acc = jnp.zeros((QBLK, 2 * QBLK), F32)
            for bk in range(N_BUCKETS):
                acc = jnp.where(bucket == bk, relb_ref[bk * 4 + h], acc)
            biasm[h * QBLK:(h + 1) * QBLK, :] = jnp.where(valid, acc * LOG2E, -jnp.inf)

    @pl.when(seq_start)
    def _reset_state():
        ubuf[0:A_PAD, :] = jnp.zeros((A_PAD, W), F32)
        cbuf[0:C_PAD, :] = jnp.zeros((C_PAD, W), F32)
        hcar[...] = jnp.zeros((SUBLANES, W), F32)
        kbuf[0:QBLK, :] = jnp.zeros((QBLK, LANES), BF16)
        vbuf[0:QBLK, :] = jnp.zeros((QBLK, LANES), BF16)
        st[...] = jnp.zeros((W, LANES), F32)

    hb = gmat = None

    proj = lambda off, width: _dot(hb, win_ref[:, off:off + width])
    shared = {}

    def project():
        pav = proj(A_OFF, 2 * W)
        ubuf[A_PAD:A_PAD + TB, :] = pav[:, 0:W] * _sigmoid(pav[:, W:2 * W])
        yield
        cbuf[C_PAD:C_PAD + TB, :] = proj(C_OFF, W)
        yield
        pb[...] = proj(B_OFF, B_QKV)
        yield
        pd[...] = proj(D_OFF, D_QKV)
        yield
        plr[...] = _dot(hb, wtail_ref[:, W:W + LANES])
        yield
        gates[:, 2 * W:3 * W] = _silu(proj(C_OFF + W, W))
        yield
        gates[:, 0:W] = _silu(proj(A_OFF + 2 * W, W))
        yield
        gates[:, W:2 * W] = _silu(proj(B_OFF + B_QKV, W))
        yield
        gates[:, 3 * W:4 * W] = _silu(_dot(hb, wtail_ref[:, 0:W]))


    def mixer_a():
        conv = jnp.broadcast_to(prow(acb_ref), (TB, W))
        conv = _causal_taps(conv, ubuf[...], aconv_ref, CONV_A, A_PAD)
        ubuf[0:A_PAD, :] = ubuf[TB:TB + A_PAD, :]
        yield
        mu = _dot_split(conv, gmat, 2)
        yield
        dc = conv - mu
        var = _dot_split(dc * dc, gmat, 1)
        yield
        un = dc * lax.rsqrt(var + EPS) * prow(alg_ref) + prow(alb_ref)
        ya = _dot(_silu(un).astype(BF16), apw_ref[...]) * gates[:, 0:W]
        ycat[:, 0:W] = ya.astype(BF16)

    def mixer_b():
        q = pb[:, 0:W]
        k = pb[:, W:W + LANES]
        v = pb[:, W + LANES:W + 2 * LANES]
        qn = q * lax.rsqrt(_dot_split(q * q, gmat, 1) + EPS)
        kn = k * lax.rsqrt(_dot_split(k * k, gmat[0:LANES, 0:LANES], 1) + EPS)
        yield
        kbuf[QBLK:QBLK + TB, :] = kn.astype(BF16)
        vbuf[QBLK:QBLK + TB, :] = v.astype(BF16)
        lane = lax.broadcasted_iota(jnp.int32, (1, LANES), 1)
        lanem = lane < HEAD
        qk_gain = prow(bqg_ref)[:, 0:LANES] * prow(bkg_ref) * (HEAD ** -0.5 * LOG2E)
        hs_bits = pltpu.bitcast(shared["hs"][TB - SUBLANES:TB, LANES:2 * LANES], jnp.int32)
        zero = (hs_bits & jnp.minimum(s_idx, 0))[0:1, :]
        qk_gain = pltpu.bitcast(pltpu.bitcast(qk_gain, jnp.int32) | zero, F32)
        lo = jnp.where(lanem, qk_gain, 0.0)
        hi = jnp.where(lanem, 0.0, qk_gain)
        first_tile_mask = jnp.where(seq_start, -jnp.inf, 0.0)

        def softmax_rows(sh, h, bi):
            if bi == 0:
                sh = jnp.concatenate([sh[:, 0:QBLK] + first_tile_mask, sh[:, QBLK:]], axis=1)
            sink = sinks_ref[layer * 4 + h] * LOG2E
            m = jnp.maximum(jnp.max(sh, axis=-1, keepdims=True), sink)
            p = jnp.exp2(sh - m)
            den = jnp.sum(p, axis=-1, keepdims=True) + jnp.exp2(sink - m)
            return p.astype(BF16), den

        for bi in range(TB // QBLK):
            r0 = bi * QBLK
            qa = qn[r0:r0 + QBLK, 0:LANES]
            qb = qn[r0:r0 + QBLK, LANES:2 * LANES]
            qst = jnp.concatenate([qa * lo, pltpu.roll(qa, HEAD, axis=1) * lo,
                                   pltpu.roll(qb, HEAD, axis=1) * hi, qb * hi], axis=0).astype(BF16)
            s = _dot_nt(qst, kbuf[r0:r0 + 2 * QBLK, :]) + biasm[...]
            yield
            pden = [softmax_rows(s[h * QBLK:(h + 1) * QBLK, :], h, bi) for h in range(4)]
            ov = _dot(jnp.concatenate([t[0] for t in pden], axis=0), vbuf[r0:r0 + 2 * QBLK, :])
            yield
            oh = [ov[h * QBLK:(h + 1) * QBLK, :] / pden[h][1] for h in range(4)]
            y01 = jnp.where(lanem, oh[0], pltpu.roll(oh[1], HEAD, axis=1))
            y23 = jnp.where(lanem, pltpu.roll(oh[2], HEAD, axis=1), oh[3])
            ycat[r0:r0 + QBLK, W:2 * W] = (jnp.concatenate([y01, y23], axis=1)
                                           * gates[r0:r0 + QBLK, W:2 * W]).astype(BF16)
        kbuf[0:QBLK, :] = kbuf[TB:TB + QBLK, :]
        vbuf[0:QBLK, :] = vbuf[TB:TB + QBLK, :]

    def mixer_c():
        xc = jnp.broadcast_to(prow(ccb_ref), (TB, W))
        xc = _causal_taps(xc, cbuf[...], ccw_ref, CONV_C, C_PAD)
        cbuf[0:C_PAD, :] = cbuf[TB:TB + C_PAD, :]
        xcb = xc.astype(BF16)
        rg = _sigmoid(_dot(xcb, cwr_ref[...]) + prow(cbr_ref))
        ig = _sigmoid(_dot(xcb, cwi_ref[...]) + prow(cbi_ref))
        yield
        nlam = -prow(lam_ref)
        softplus = jnp.maximum(nlam, 0.0) + jnp.log1p(jnp.exp(-jnp.abs(nlam)))
        a = jnp.exp2(rg * (softplus * (-LRU_C * LOG2E)))
        gap = 1.0 - a * a
        root = jnp.where(gap > 0.0, gap * lax.rsqrt(gap), 0.0)
        uu = root * (ig * xc)
        groups = TB // SUBLANES
        a3 = a.reshape(groups, SUBLANES, W)
        u3 = uu.reshape(groups, SUBLANES, W)
        row = lax.broadcasted_iota(jnp.int32, (1, SUBLANES, W), 1)
        step = 1
        while step < SUBLANES:
            inside = row >= step
            u3 = a3 * jnp.where(inside, pltpu.roll(u3, step, axis=1), 0.0) + u3
            a3 = a3 * jnp.where(inside, pltpu.roll(a3, step, axis=1), 1.0)
            step *= 2
        h_in = hcar[0:1, :]
        h_groups = []
        for g in range(groups):
            h_g = u3[g] + a3[g] * h_in
            h_groups.append(h_g)
            h_in = h_g[SUBLANES - 1:SUBLANES, :]
        hs = jnp.concatenate(h_groups, axis=0)
        hcar[0:1, :] = h_in
        shared["hs"] = hs
        ycat[:, 2 * W:3 * W] = (hs * gates[:, 2 * W:3 * W]).astype(BF16)

    def mixer_d():
        dq = pd[:, 0:LANES] * (D_KEY ** -0.5)
        dk = pd[:, LANES:2 * LANES]
        dv = pd[:, 2 * LANES:2 * LANES + W]
        z = _dot(plr[...].astype(BF16), dwup[...]) + prow(dbu_ref)
        yield
        lg = (jnp.minimum(z, 0.0) - jnp.log1p(jnp.exp2(jnp.abs(z) * -LOG2E))) * (LOG2E / GATE_TAU)
        lg_hi = lg.astype(BF16)
        lg_lo = (lg - lg_hi.astype(F32)).astype(BF16)
        pieces = jnp.concatenate([lg_hi, lg_lo], axis=1)
        both = jnp.concatenate([_dot(tri[...], pieces[r0:r0 + CUM_ROWS]) for r0 in range(0, TB, CUM_ROWS)], axis=0)
        bcum = both[:, 0:LANES] + both[:, LANES:2 * LANES]
        yield
        lane = lax.broadcasted_iota(jnp.int32, (CHUNK, LANES), 1)
        lane256 = lax.broadcasted_iota(jnp.int32, (CHUNK, W), 1)
        key_masks = [(lane // D_KEY == h).astype(F32).astype(BF16) for h in range(D_HEADS)]
        val_masks = [(lane256 // HEAD == h).astype(F32).astype(BF16) for h in range(D_HEADS)]
        ci = lax.broadcasted_iota(jnp.int32, (CHUNK, W), 0)
        cj = lax.broadcasted_iota(jnp.int32, (CHUNK, W), 1) % CHUNK
        causal = (ci >= cj).astype(F32).astype(BF16)
        bdm = (lax.broadcasted_iota(jnp.int32, (W, LANES), 0) // HEAD
               == lax.broadcasted_iota(jnp.int32, (W, LANES), 1) // D_KEY).astype(F32)
        state = st[...]
        od_parts = []
        for c in range(TB // CHUNK):
            r0 = c * CHUNK
            bc = bcum[r0:r0 + CHUNK, :]
            bl = bc[CHUNK - 1:CHUNK, :]
            qt = (dq[r0:r0 + CHUNK, :] * jnp.exp2(bc)).astype(BF16)
            kc = dk[r0:r0 + CHUNK, :]
            kt = kc * jnp.exp2(-bc)
            ke = (kc * jnp.exp2(bl - bc)).astype(BF16)
            vc = dv[r0:r0 + CHUNK, :]
            kt_bf = kt.astype(BF16)
            kst = jnp.concatenate([kt_bf * key_masks[h] for h in range(D_HEADS)],
                                  axis=0)
            att = _dot_nt(qt, kst).astype(BF16) * causal
            yield
            vc_bf = vc.astype(BF16)
            vbd = jnp.concatenate([vc_bf * val_masks[h] for h in range(D_HEADS)],
                                  axis=0)
            o_c = _dot(att, vbd) + _dot_nt(qt, state.astype(BF16))
            state = state * jnp.exp2(bl) + _dot_tn(vc_bf, ke) * bdm
            od_parts.append(o_c)
            yield
        st[...] = state
        od = jnp.concatenate(od_parts, axis=0)
        odn = od * lax.rsqrt(_dot_split(od * od, gmat, 1) + EPS) * prow(dng_ref)
        ycat[:, 3 * W:4 * W] = (odn * gates[:, 3 * W:4 * W]).astype(BF16)

    def out_proj_prev():
        for c0 in range(0, D_MODEL, OUT_SLAB):
            o_ref[0, :, c0:c0 + OUT_SLAB] = (xo_ref[0, :, c0:c0 + OUT_SLAB]
                                             + _dot(yprev[...], wout_ref[:, c0:c0 + OUT_SLAB]))
            yield

    @pl.when(s_idx < n_tiles)
    def main_step():
        nonlocal hb, gmat
        x = x_ref[0]
        ms = jnp.mean(x * x, axis=-1, keepdims=True)
        hb = (x * lax.rsqrt(ms + EPS) * prow(ng_ref)).astype(BF16)
        gmat = g256[...]
        gens = {"A": mixer_a(), "B": mixer_b(), "C": mixer_c(), "D": mixer_d(), "O": out_proj_prev(),
                "P": project()}
        for name in TRACE_ORDER:
            next(gens[name], None)
        running = [gens[k] for k in "PDABCO"]
        while running:
            running = [g for g in running if next(g, StopIteration) is not StopIteration]
        yprev[...] = ycat[...]

    @pl.when(s_idx == n_tiles)
    def last_step():
        for _ in out_proj_prev():
            pass


def _t5_bucket(dist):
    max_exact = N_BUCKETS // 2
    d = jnp.maximum(dist, 1).astype(F32)
    large = max_exact + (jnp.log(d / max_exact) / math.log(MAX_DISTANCE / max_exact)
                         * (N_BUCKETS - max_exact)).astype(jnp.int32)
    large = jnp.minimum(large, N_BUCKETS - 1)
    return jnp.where(dist < max_exact, dist, large)


def _layer_spec(layer, shape):
    return pl.BlockSpec((None,) + shape, lambda s: (layer,) + (0,) * len(shape))


def _whole(shape):
    return pl.BlockSpec(shape, lambda s: (0,) * len(shape))


def _layer_call(layer, tiles_per_seq, x, ng, win, wtail, wout, apw, cwr, cwi, dwup, aconv, vec256, ccw, dbu,
                dng, bqg, bkg, bucket, sinks, relb):
    n_tiles = x.shape[0]
    smem = pl.BlockSpec(memory_space=pltpu.SMEM)
    spec = functools.partial(_layer_spec, layer)
    acb, alg, alb, ccb, cbr, cbi, lam = vec256
    prev_tile = lambda s: (jnp.maximum(s - 1, 0), 0, 0)
    return pl.pallas_call(
        functools.partial(_layer_kernel, layer, tiles_per_seq, n_tiles),
        grid=(n_tiles + 1,),
        in_specs=[
            pl.BlockSpec((1, TB, D_MODEL), lambda s: (jnp.minimum(s, n_tiles - 1), 0, 0)),
            pl.BlockSpec((1, TB, D_MODEL), prev_tile),
            _whole((DEPTH, D_MODEL)), spec((D_MODEL, IN_COLS)), spec((D_MODEL, D_TAIL)),
            spec((D_MODEL, D_MODEL)), spec((W, W)), spec((W, W)), spec((W, W)),
            spec((GATE_RANK, LANES)), spec((CONV_A, W)),
            _whole((DEPTH, W)), _whole((DEPTH, W)), _whole((DEPTH, W)),
            spec((CONV_C, W)),
            _whole((DEPTH, W)), _whole((DEPTH, W)), _whole((DEPTH, W)), _whole((DEPTH, W)),
            _whole((DEPTH, LANES)),
            _whole((DEPTH, W)), _whole((DEPTH, W)), _whole((DEPTH, LANES)),
            _whole((QBLK, 2 * QBLK)),
            smem, smem,
        ],
        out_specs=pl.BlockSpec((1, TB, D_MODEL), prev_tile),
        out_shape=jax.ShapeDtypeStruct(x.shape, F32),
        scratch_shapes=[
            pltpu.VMEM((TB + A_PAD, W), F32),
            pltpu.VMEM((TB + C_PAD, W), F32),
            pltpu.VMEM((SUBLANES, W), F32),
            pltpu.VMEM((TB + QBLK, LANES), BF16),
            pltpu.VMEM((TB + QBLK, LANES), BF16),
            pltpu.VMEM((W, LANES), F32),
            pltpu.VMEM((TB, D_MODEL), BF16),
            pltpu.VMEM((4 * QBLK, 2 * QBLK), F32),
            pltpu.VMEM((W, W), BF16),
            pltpu.VMEM((CUM_ROWS, CUM_ROWS), BF16),
            pltpu.VMEM((LANES, LANES), BF16),
            pltpu.VMEM((TB, D_MODEL), F32),
            pltpu.VMEM((TB, B_QKV), F32),
            pltpu.VMEM((TB, D_QKV), F32),
            pltpu.VMEM((TB, LANES), F32),
            pltpu.VMEM((TB, D_MODEL), BF16),
        ],
        compiler_params=pltpu.CompilerParams(
            dimension_semantics=("arbitrary",),
            vmem_limit_bytes=VMEM_LIMIT_BYTES,
        ),
        name="hybrid_layer",
    )(x, x, ng, win, wtail, wout, apw, cwr, cwi, dwup, aconv, acb, alg, alb, ccw, ccb, cbr, cbi, lam, dbu,
      dng, bqg, bkg, bucket, sinks, relb)


def _block_diag(blocks):
    depth, n, c, _ = blocks.shape
    eye = jnp.eye(n, dtype=blocks.dtype)
    return (eye[None, :, None, :, None] * blocks[:, :, :, None, :]).reshape(depth, n * c, n * c)


def kernel(x, norm_g, w_in, a_conv_w, a_conv_b, a_ln_g, a_ln_b, a_pw, b_q_g, b_k_g, b_sinks, rel_bias,
           c_conv_w, c_conv_b, c_w_r, c_b_r, c_w_i, c_b_i, c_lambda, d_w_up, d_b_up, d_norm_g, w_out):
    dist = jnp.arange(QBLK)[:, None] + QBLK - jnp.arange(2 * QBLK)[None, :]
    bucket = _t5_bucket(jnp.clip(dist, 0, None)).astype(jnp.int32)
    wtail = jnp.concatenate([
        w_in[:, :, D_TAIL_SRC + GATE_RANK:IN_COLS], w_in[:, :, D_TAIL_SRC:D_TAIL_SRC + GATE_RANK],
        jnp.zeros((DEPTH, D_MODEL, D_TAIL - (IN_COLS - D_TAIL_SRC)), F32)], axis=2).astype(BF16)
    vec256 = (a_conv_b, a_ln_g, a_ln_b, c_conv_b, c_b_r, c_b_i, c_lambda)
    args = (norm_g, w_in.astype(BF16), wtail, w_out.astype(BF16), a_pw.astype(BF16),
            _block_diag(c_w_r).astype(BF16), _block_diag(c_w_i).astype(BF16), d_w_up, a_conv_w,
            vec256, c_conv_w, d_b_up, jnp.tile(d_norm_g, (1, 4)), jnp.tile(b_q_g, (1, 4)),
            jnp.tile(b_k_g, (1, 2)), bucket, b_sinks.reshape(-1), rel_bias.reshape(-1))
    bsz, seq, _ = x.shape
    assert seq % TB == 0
    h = x.reshape(bsz * seq // TB, TB, D_MODEL)
    for l in range(DEPTH):
        h = _layer_call(l, seq // TB, h, *args)
    return h.reshape(bsz, seq, D_MODEL)
```

```python
import functools
import math

import jax
import jax.numpy as jnp
from jax import lax
from jax.experimental import pallas as pl
from jax.experimental.pallas import tpu as pltpu

F32 = jnp.float32
BF16 = jnp.bfloat16

D_MODEL = 1024
DEPTH = 4
W = 256
EPS = 1e-6
CONV_A = 31
A_PAD = 32
HEAD = 64
WINDOW = 128
QBLK = 128
N_BUCKETS = 32
MAX_DISTANCE = 128
CONV_C = 4
C_PAD = 8
LRU_C = 8.0
D_HEADS = 4
D_KEY = 32
GATE_RANK = 16
GATE_TAU = 16.0
CHUNK = 64
SUBLANES = 8
LANES = 128
VMEM_LIMIT_BYTES = 48 * 1024 * 1024
LOG2E = math.log2(math.e)

TB = 512
TRACE_ORDER = ("O P A P P C P P O B D P C O D B P D D B D D O P D D D D B D D D D B D D D D B P A B B B A A "
               "D B").split()
OUT_SLAB = 256
CUM_ROWS = 256
TAP_ROWS = 64

A_OFF = 0
B_OFF = 768
B_QKV = 512
C_OFF = 1536
D_OFF, D_QKV = 2048, 512
D_TAIL_SRC = D_OFF + D_QKV
D_TAIL = 384
IN_COLS = 2832


def _dot(a, b):
    return jnp.dot(a, b, preferred_element_type=F32)


def _dot_nt(a, b):
    return lax.dot_general(a, b, (((1,), (1,)), ((), ())), preferred_element_type=F32)


def _dot_tn(a, b):
    return lax.dot_general(a, b, (((0,), (0,)), ((), ())), preferred_element_type=F32)


def _dot_split(x, w_bf, passes):
    acc = None
    r = x
    for p in range(passes):
        part = r.astype(BF16)
        d = _dot(part, w_bf)
        acc = d if acc is None else acc + d
        if p + 1 < passes:
            r = r - part.astype(F32)
    return acc


def _sigmoid(x):
    return 1.0 / (1.0 + jnp.exp2(x * -LOG2E))


def _silu(x):
    return x * _sigmoid(x)


def _causal_taps(acc, buf, w_ref, n_taps, pad):
    rows = buf.shape[0]
    offs = [pad - (n_taps - 1) + j for j in range(n_taps)]
    shifted = {res: buf if res == 0 else pltpu.roll(buf, rows - res, axis=0)
               for res in sorted({off % SUBLANES for off in offs})}
    out = []
    for r0 in range(0, TB, TAP_ROWS):
        part = acc[r0:r0 + TAP_ROWS]
        for res in shifted:
            for j, off in enumerate(offs):
                if off % SUBLANES == res:
                    base = off - res + r0
                    part = part + w_ref[j:j + 1, :] * shifted[res][base:base + TAP_ROWS]
        out.append(part)
    return jnp.concatenate(out, axis=0)


def _shift_rows(x, step, fill):
    if step % SUBLANES == 0:
        return jnp.concatenate([jnp.full((step, x.shape[1]), fill, F32), x[0:x.shape[0] - step]], axis=0)
    rolled = pltpu.roll(x, step, axis=0)
    row = lax.broadcasted_iota(jnp.int32, (SUBLANES, x.shape[1]), 0)
    head = jnp.where(row < step, fill, rolled[0:SUBLANES])
    return jnp.concatenate([head, rolled[SUBLANES:]], axis=0)


def _layer_kernel(layer, tiles_per_seq, n_tiles,
                  x_ref, xo_ref, ng_ref, win_ref, wtail_ref, wout_ref, apw_ref, cwr_ref, cwi_ref, dwup_ref, aconv_ref,
                  acb_ref, alg_ref, alb_ref, ccw_ref, ccb_ref, cbr_ref, cbi_ref, lam_ref, dbu_ref,
                  dng_ref, bqg_ref, bkg_ref, bucket_ref, sinks_ref, relb_ref,
                  o_ref,
                  ubuf, cbuf, hcar, kbuf, vbuf, st, ycat, biasm, g256, tri, dwup, gates, pb, pd, plr, yprev):
    s_idx = pl.program_id(0)
    seq_start = (s_idx % tiles_per_seq) == 0
    prow = lambda ref: ref[layer:layer + 1, :]

    @pl.when(s_idx == 0)
    def _build_tables():
        yprev[...] = jnp.zeros((TB, D_MODEL), BF16)
        ri = lax.broadcasted_iota(jnp.int32, (W, W), 0) // HEAD
        ci = lax.broadcasted_iota(jnp.int32, (W, W), 1) // HEAD
        g256[...] = jnp.where(ri == ci, 1.0 / HEAD, 0.0).astype(BF16)
        rt = lax.broadcasted_iota(jnp.int32, (CUM_ROWS, CUM_ROWS), 0)
        ct = lax.broadcasted_iota(jnp.int32, (CUM_ROWS, CUM_ROWS), 1)
        tri[...] = jnp.where((rt // CHUNK == ct // CHUNK) & (rt >= ct), 1.0, 0.0).astype(BF16)
        dwup[...] = jnp.zeros((LANES, LANES), BF16)
        dwup[0:GATE_RANK, :] = dwup_ref[...].astype(BF16)
        bucket = bucket_ref[...]
        qi = lax.broadcasted_iota(jnp.int32, (QBLK, 2 * QBLK), 0)
        kj = lax.broadcasted_iota(jnp.int32, (QBLK, 2 * QBLK), 1)
        dist = qi + QBLK - kj
        valid = (dist >= 0) & (dist < WINDOW)
        accs = [jnp.zeros((QBLK, 2 * QBLK), F32) for _ in range(4)]
        for bk in range(N_BUCKETS):
            hit = bucket == bk
            accs = [jnp.where(hit, relb_ref[bk * 4 + h], accs[h]) for h in range(4)]
        for h in range(4):
            biasm[h * QBLK:(h + 1) * QBLK, :] = jnp.where(valid, accs[h] * LOG2E, -jnp.inf)

    @pl.when(seq_start)
    def _reset_state():
        ubuf[0:A_PAD, :] = jnp.zeros((A_PAD, W), F32)
        cbuf[0:C_PAD, :] = jnp.zeros((C_PAD, W), F32)
        hcar[...] = jnp.zeros((SUBLANES, W), F32)
        kbuf[0:QBLK, :] = jnp.zeros((QBLK, LANES), BF16)
        vbuf[0:QBLK, :] = jnp.zeros((QBLK, LANES), BF16)
        st[...] = jnp.zeros((W, LANES), F32)

    hb = gmat = None

    proj = lambda off, width: _dot(hb, win_ref[:, off:off + width])
    shared = {}

    def project():
        pav = proj(A_OFF, 2 * W)
        ubuf[A_PAD:A_PAD + TB, :] = pav[:, 0:W] * _sigmoid(pav[:, W:2 * W])
        yield
        cbuf[C_PAD:C_PAD + TB, :] = proj(C_OFF, W)
        yield
        pb[...] = proj(B_OFF, B_QKV)
        yield
        pd[...] = proj(D_OFF, D_QKV)
        yield
        plr[...] = _dot(hb, wtail_ref[:, W:W + LANES])
        yield
        gates[:, 2 * W:3 * W] = _silu(proj(C_OFF + W, W))
        yield
        gates[:, 0:W] = _silu(proj(A_OFF + 2 * W, W))
        yield
        gates[:, W:2 * W] = _silu(proj(B_OFF + B_QKV, W))
        yield
        gates[:, 3 * W:4 * W] = _silu(_dot(hb, wtail_ref[:, 0:W]))


    def mixer_a():
        conv = jnp.broadcast_to(prow(acb_ref), (TB, W))
        conv = _causal_taps(conv, ubuf[...], aconv_ref, CONV_A, A_PAD)
        ubuf[0:A_PAD, :] = ubuf[TB:TB + A_PAD, :]
        yield
        mu = _dot_split(conv, gmat, 2)
        yield
        dc = conv - mu
        var = _dot_split(dc * dc, gmat, 1)
        yield
        un = dc * lax.rsqrt(var + EPS) * prow(alg_ref) + prow(alb_ref)
        ya = _dot(_silu(un).astype(BF16), apw_ref[...]) * gates[:, 0:W]
        ycat[:, 0:W] = ya.astype(BF16)

    def mixer_b():
        q = pb[:, 0:W]
        k = pb[:, W:W + LANES]
        v = pb[:, W + LANES:W + 2 * LANES]
        qn = q * lax.rsqrt(_dot_split(q * q, gmat, 1) + EPS)
        kn = k * lax.rsqrt(_dot_split(k * k, gmat[0:LANES, 0:LANES], 1) + EPS)
        yield
        kbuf[QBLK:QBLK + TB, :] = kn.astype(BF16)
        vbuf[QBLK:QBLK + TB, :] = v.astype(BF16)
        lane = lax.broadcasted_iota(jnp.int32, (1, LANES), 1)
        lanem = lane < HEAD
        qk_gain = prow(bqg_ref)[:, 0:LANES] * prow(bkg_ref) * (HEAD ** -0.5 * LOG2E)
        hs_bits = pltpu.bitcast(shared["hs"][TB - SUBLANES:TB, LANES:2 * LANES], jnp.int32)
        zero = (hs_bits & jnp.minimum(s_idx, 0))[0:1, :]
        qk_gain = pltpu.bitcast(pltpu.bitcast(qk_gain, jnp.int32) | zero, F32)
        lo = jnp.where(lanem, qk_gain, 0.0)
        hi = jnp.where(lanem, 0.0, qk_gain)
        first_tile_mask = jnp.where(seq_start, -jnp.inf, 0.0)

        def softmax_rows(sh, h, bi):
            if bi == 0:
                sh = jnp.concatenate([sh[:, 0:QBLK] + first_tile_mask, sh[:, QBLK:]], axis=1)
            sink = sinks_ref[layer * 4 + h] * LOG2E
            m = jnp.maximum(jnp.max(sh, axis=-1, keepdims=True), sink)
            p = jnp.exp2(sh - m)
            den = jnp.sum(p, axis=-1, keepdims=True) + jnp.exp2(sink - m)
            return p.astype(BF16), den

        for bi in range(TB // QBLK):
            r0 = bi * QBLK
            qa = qn[r0:r0 + QBLK, 0:LANES]
            qb = qn[r0:r0 + QBLK, LANES:2 * LANES]
            qst = jnp.concatenate([qa * lo, pltpu.roll(qa, HEAD, axis=1) * lo,
                                   pltpu.roll(qb, HEAD, axis=1) * hi, qb * hi], axis=0).astype(BF16)
            s = _dot_nt(qst, kbuf[r0:r0 + 2 * QBLK, :]) + biasm[...]
            yield
            pden = [softmax_rows(s[h * QBLK:(h + 1) * QBLK, :], h, bi) for h in range(4)]
            ov = _dot(jnp.concatenate([t[0] for t in pden], axis=0), vbuf[r0:r0 + 2 * QBLK, :])
            yield
            oh = [ov[h * QBLK:(h + 1) * QBLK, :] / pden[h][1] for h in range(4)]
            y01 = jnp.where(lanem, oh[0], pltpu.roll(oh[1], HEAD, axis=1))
            y23 = jnp.where(lanem, pltpu.roll(oh[2], HEAD, axis=1), oh[3])
            ycat[r0:r0 + QBLK, W:2 * W] = (jnp.concatenate([y01, y23], axis=1)
                                           * gates[r0:r0 + QBLK, W:2 * W]).astype(BF16)
        kbuf[0:QBLK, :] = kbuf[TB:TB + QBLK, :]
        vbuf[0:QBLK, :] = vbuf[TB:TB + QBLK, :]

    def mixer_c():
        xc = jnp.broadcast_to(prow(ccb_ref), (TB, W))
        xc = _causal_taps(xc, cbuf[...], ccw_ref, CONV_C, C_PAD)
        cbuf[0:C_PAD, :] = cbuf[TB:TB + C_PAD, :]
        xcb = xc.astype(BF16)
        rg = _sigmoid(_dot(xcb, cwr_ref[...]) + prow(cbr_ref))
        ig = _sigmoid(_dot(xcb, cwi_ref[...]) + prow(cbi_ref))
        yield
        nlam = -prow(lam_ref)
        softplus = jnp.maximum(nlam, 0.0) + jnp.log1p(jnp.exp(-jnp.abs(nlam)))
        a = jnp.exp2(rg * (softplus * (-LRU_C * LOG2E)))
        gap = 1.0 - a * a
        root = jnp.where(gap > 0.0, gap * lax.rsqrt(gap), 0.0)
        uu = root * (ig * xc)
        groups = TB // SUBLANES
        a3 = a.reshape(groups, SUBLANES, W)
        u3 = uu.reshape(groups, SUBLANES, W)
        row = lax.broadcasted_iota(jnp.int32, (1, SUBLANES, W), 1)
        step = 1
        while step < SUBLANES:
            inside = row >= step
            u3 = a3 * jnp.where(inside, pltpu.roll(u3, step, axis=1), 0.0) + u3
            a3 = a3 * jnp.where(inside, pltpu.roll(a3, step, axis=1), 1.0)
            step *= 2
        h_in = hcar[0:1, :]
        h_groups = []
        for g in range(groups):
            h_g = u3[g] + a3[g] * h_in
            h_groups.append(h_g)
            h_in = h_g[SUBLANES - 1:SUBLANES, :]
        hs = jnp.concatenate(h_groups, axis=0)
        hcar[0:1, :] = h_in
        shared["hs"] = hs
        ycat[:, 2 * W:3 * W] = (hs * gates[:, 2 * W:3 * W]).astype(BF16)

    def mixer_d():
        dq = pd[:, 0:LANES] * (D_KEY ** -0.5)
        dk = pd[:, LANES:2 * LANES]
        dv = pd[:, 2 * LANES:2 * LANES + W]
        z = _dot(plr[...].astype(BF16), dwup[...]) + prow(dbu_ref)
        yield
        lg = (jnp.minimum(z, 0.0) - jnp.log1p(jnp.exp2(jnp.abs(z) * -LOG2E))) * (LOG2E / GATE_TAU)
        lg_hi = lg.astype(BF16)
        lg_lo = (lg - lg_hi.astype(F32)).astype(BF16)
        pieces = jnp.concatenate([lg_hi, lg_lo], axis=1)
        both = jnp.concatenate([_dot(tri[...], pieces[r0:r0 + CUM_ROWS]) for r0 in range(0, TB, CUM_ROWS)], axis=0)
        bcum = both[:, 0:LANES] + both[:, LANES:2 * LANES]
        yield
        lane = lax.broadcasted_iota(jnp.int32, (CHUNK, LANES), 1)
        lane256 = lax.broadcasted_iota(jnp.int32, (CHUNK, W), 1)
        key_masks = [(lane // D_KEY == h).astype(F32).astype(BF16) for h in range(D_HEADS)]
        val_masks = [(lane256 // HEAD == h).astype(F32).astype(BF16) for h in range(D_HEADS)]
        ci = lax.broadcasted_iota(jnp.int32, (CHUNK, W), 0)
        cj = lax.broadcasted_iota(jnp.int32, (CHUNK, W), 1) % CHUNK
        causal = (ci >= cj).astype(F32).astype(BF16)
        bdm = (lax.broadcasted_iota(jnp.int32, (W, LANES), 0) // HEAD
               == lax.broadcasted_iota(jnp.int32, (W, LANES), 1) // D_KEY).astype(F32)
        state = st[...]
        od_parts = []
        for c in range(TB // CHUNK):
            r0 = c * CHUNK
            bc = bcum[r0:r0 + CHUNK, :]
            bl = bc[CHUNK - 1:CHUNK, :]
            qt = (dq[r0:r0 + CHUNK, :] * jnp.exp2(bc)).astype(BF16)
            kc = dk[r0:r0 + CHUNK, :]
            kt = kc * jnp.exp2(-bc)
            ke = (kc * jnp.exp2(bl - bc)).astype(BF16)
            vc = dv[r0:r0 + CHUNK, :]
            kt_bf = kt.astype(BF16)
            kst = jnp.concatenate([kt_bf * key_masks[h] for h in range(D_HEADS)],
                                  axis=0)
            att = _dot_nt(qt, kst).astype(BF16) * causal
            yield
            vc_bf = vc.astype(BF16)
            vbd = jnp.concatenate([vc_bf * val_masks[h] for h in range(D_HEADS)],
                                  axis=0)
            o_c = _dot(att, vbd) + _dot_nt(qt, state.astype(BF16))
            state = state * jnp.exp2(bl) + _dot_tn(vc_bf, ke) * bdm
            od_parts.append(o_c)
            yield
        st[...] = state
        od = jnp.concatenate(od_parts, axis=0)
        odn = od * lax.rsqrt(_dot_split(od * od, gmat, 1) + EPS) * prow(dng_ref)
        ycat[:, 3 * W:4 * W] = (odn * gates[:, 3 * W:4 * W]).astype(BF16)

    def out_proj_prev():
        for c0 in range(0, D_MODEL, OUT_SLAB):
            o_ref[0, :, c0:c0 + OUT_SLAB] = (xo_ref[0, :, c0:c0 + OUT_SLAB]
                                             + _dot(yprev[...], wout_ref[:, c0:c0 + OUT_SLAB]))
            yield

    @pl.when(s_idx < n_tiles)
    def main_step():
        nonlocal hb, gmat
        x = x_ref[0]
        ms = jnp.mean(x * x, axis=-1, keepdims=True)
        hb = (x * lax.rsqrt(ms + EPS) * prow(ng_ref)).astype(BF16)
        gmat = g256[...]
        gens = {"A": mixer_a(), "B": mixer_b(), "C": mixer_c(), "D": mixer_d(), "O": out_proj_prev(),
                "P": project()}
        for name in TRACE_ORDER:
            next(gens[name], None)
        running = [gens[k] for k in "PDABCO"]
        while running:
            running = [g for g in running if next(g, StopIteration) is not StopIteration]
        yprev[...] = ycat[...]

    @pl.when(s_idx == n_tiles)
    def last_step():
        for _ in out_proj_prev():
            pass


def _t5_bucket(dist):
    max_exact = N_BUCKETS // 2
    d = jnp.maximum(dist, 1).astype(F32)
    large = max_exact + (jnp.log(d / max_exact) / math.log(MAX_DISTANCE / max_exact)
                         * (N_BUCKETS - max_exact)).astype(jnp.int32)
    large = jnp.minimum(large, N_BUCKETS - 1)
    return jnp.where(dist < max_exact, dist, large)


def _layer_spec(layer, shape):
    return pl.BlockSpec((None,) + shape, lambda s: (layer,) + (0,) * len(shape))


def _whole(shape):
    return pl.BlockSpec(shape, lambda s: (0,) * len(shape))


def _layer_call(layer, tiles_per_seq, x, ng, win, wtail, wout, apw, cwr, cwi, dwup, aconv, vec256, ccw, dbu,
                dng, bqg, bkg, bucket, sinks, relb):
    n_tiles = x.shape[0]
    smem = pl.BlockSpec(memory_space=pltpu.SMEM)
    spec = functools.partial(_layer_spec, layer)
    acb, alg, alb, ccb, cbr, cbi, lam = vec256
    prev_tile = lambda s: (jnp.maximum(s - 1, 0), 0, 0)
    return pl.pallas_call(
        functools.partial(_layer_kernel, layer, tiles_per_seq, n_tiles),
        grid=(n_tiles + 1,),
        in_specs=[
            pl.BlockSpec((1, TB, D_MODEL), lambda s: (jnp.minimum(s, n_tiles - 1), 0, 0)),
            pl.BlockSpec((1, TB, D_MODEL), prev_tile),
            _whole((DEPTH, D_MODEL)), spec((D_MODEL, IN_COLS)), spec((D_MODEL, D_TAIL)),
            spec((D_MODEL, D_MODEL)), spec((W, W)), spec((W, W)), spec((W, W)),
            spec((GATE_RANK, LANES)), spec((CONV_A, W)),
            _whole((DEPTH, W)), _whole((DEPTH, W)), _whole((DEPTH, W)),
            spec((CONV_C, W)),
            _whole((DEPTH, W)), _whole((DEPTH, W)), _whole((DEPTH, W)), _whole((DEPTH, W)),
            _whole((DEPTH, LANES)),
            _whole((DEPTH, W)), _whole((DEPTH, W)), _whole((DEPTH, LANES)),
            _whole((QBLK, 2 * QBLK)),
            smem, smem,
        ],
        out_specs=pl.BlockSpec((1, TB, D_MODEL), prev_tile),
        out_shape=jax.ShapeDtypeStruct(x.shape, F32),
        scratch_shapes=[
            pltpu.VMEM((TB + A_PAD, W), F32),
            pltpu.VMEM((TB + C_PAD, W), F32),
            pltpu.VMEM((SUBLANES, W), F32),
            pltpu.VMEM((TB + QBLK, LANES), BF16),
            pltpu.VMEM((TB + QBLK, LANES), BF16),
            pltpu.VMEM((W, LANES), F32),
            pltpu.VMEM((TB, D_MODEL), BF16),
            pltpu.VMEM((4 * QBLK, 2 * QBLK), F32),
            pltpu.VMEM((W, W), BF16),
            pltpu.VMEM((CUM_ROWS, CUM_ROWS), BF16),
            pltpu.VMEM((LANES, LANES), BF16),
            pltpu.VMEM((TB, D_MODEL), F32),
            pltpu.VMEM((TB, B_QKV), F32),
            pltpu.VMEM((TB, D_QKV), F32),
            pltpu.VMEM((TB, LANES), F32),
            pltpu.VMEM((TB, D_MODEL), BF16),
        ],
        compiler_params=pltpu.CompilerParams(
            dimension_semantics=("arbitrary",),
            vmem_limit_bytes=VMEM_LIMIT_BYTES,
        ),
        name="hybrid_layer",
    )(x, x, ng, win, wtail, wout, apw, cwr, cwi, dwup, aconv, acb, alg, alb, ccw, ccb, cbr, cbi, lam, dbu,
      dng, bqg, bkg, bucket, sinks, relb)


def _block_diag(blocks):
    depth, n, c, _ = blocks.shape
    eye = jnp.eye(n, dtype=blocks.dtype)
    return (eye[None, :, None, :, None] * blocks[:, :, :, None, :]).reshape(depth, n * c, n * c)


def kernel(x, norm_g, w_in, a_conv_w, a_conv_b, a_ln_g, a_ln_b, a_pw, b_q_g, b_k_g, b_sinks, rel_bias,
           c_conv_w, c_conv_b, c_w_r, c_b_r, c_w_i, c_b_i, c_lambda, d_w_up, d_b_up, d_norm_g, w_out):
    dist = jnp.arange(QBLK)[:, None] + QBLK - jnp.arange(2 * QBLK)[None, :]
    bucket = _t5_bucket(jnp.clip(dist, 0, None)).astype(jnp.int32)
    wtail = jnp.concatenate([
        w_in[:, :, D_TAIL_SRC + GATE_RANK:IN_COLS], w_in[:, :, D_TAIL_SRC:D_TAIL_SRC + GATE_RANK],
        jnp.zeros((DEPTH, D_MODEL, D_TAIL - (IN_COLS - D_TAIL_SRC)), F32)], axis=2).astype(BF16)
    vec256 = (a_conv_b, a_ln_g, a_ln_b, c_conv_b, c_b_r, c_b_i, c_lambda)
    args = (norm_g, w_in.astype(BF16), wtail, w_out.astype(BF16), a_pw.astype(BF16),
            _block_diag(c_w_r).astype(BF16), _block_diag(c_w_i).astype(BF16), d_w_up, a_conv_w,
            vec256, c_conv_w, d_b_up, jnp.tile(d_norm_g, (1, 4)), jnp.tile(b_q_g, (1, 4)),
            jnp.tile(b_k_g, (1, 2)), bucket, b_sinks.reshape(-1), rel_bias.reshape(-1))
    bsz, seq, _ = x.shape
    assert seq % TB == 0
    h = x.reshape(bsz * seq // TB, TB, D_MODEL)
    for l in range(DEPTH):
        h = _layer_call(l, seq // TB, h, *args)
    return h.reshape(bsz, seq, D_MODEL)
```
